```python
import math
import jax, jax.numpy as jnp
from jax import lax
import numpy as np

D_MODEL = 2048
BATCH = 2
SEQ = 4096
DEPTH = 2
DEC_BATCH = 32
DEC_SEQ = 4
PAST_LEN = 8192
PAGE_SIZE = 128

HEAD_DIM = 128
N_HEADS = D_MODEL // HEAD_DIM
N_KV_HEADS = N_HEADS // 4
GROUP = N_HEADS // N_KV_HEADS
Q_DIM = N_HEADS * HEAD_DIM
KV_DIM = N_KV_HEADS * HEAD_DIM
QKV_DIM = Q_DIM + 2 * KV_DIM
D_FF = 4 * D_MODEL
N_MIXERS = 2
N_FOX_LAYERS = (DEPTH + 1) // 2
FOX_Q_BLOCK = 128
MOBA_BLOCK = 256
MOBA_TOPK = 3
MOBA_Q_BLOCK = 32
FGATE_BIAS = 3.0
RMS_EPS = 1e-6
F32 = jnp.float32

kernel_name = "fox_moba_hybrid_decode_step"


def rms_norm(x, g):
    xf = x.astype(F32)
    y = xf * lax.rsqrt(jnp.mean(xf * xf, axis=-1, keepdims=True) + RMS_EPS)
    return (y * g.astype(F32)).astype(x.dtype)


def project_qkv(xn, w_qkv, q_gain, k_gain):
    b, t, _ = xn.shape
    qkv = xn @ w_qkv
    q = qkv[..., :Q_DIM].reshape(b, t, N_HEADS, HEAD_DIM)
    k = qkv[..., Q_DIM:Q_DIM + KV_DIM].reshape(b, t, N_KV_HEADS, HEAD_DIM)
    v = qkv[..., Q_DIM + KV_DIM:].reshape(b, t, N_KV_HEADS, HEAD_DIM)
    return rms_norm(q, q_gain), rms_norm(k, k_gain), v


def gather_past(pool, page_table):
    g = pool[page_table]
    return g.reshape((page_table.shape[0], page_table.shape[1] * pool.shape[1]) + pool.shape[2:])


def fox_attention(q, k, v, c_q, c_k, q_offset, q_block):
    b, tq = q.shape[:2]
    L = k.shape[1]
    nb = tq // q_block
    qb = q.reshape(b, nb, q_block, N_KV_HEADS, GROUP, HEAD_DIM).swapaxes(0, 1)
    cqb = c_q.reshape(b, nb, q_block, N_KV_HEADS, GROUP).transpose(1, 0, 3, 4, 2)
    ck = c_k.reshape(b, L, N_KV_HEADS, GROUP).transpose(0, 2, 3, 1)
    k_pos = jnp.arange(L)
    starts = q_offset + jnp.arange(nb) * q_block
    scale = HEAD_DIM ** -0.5

    def one_block(args):
        qi, cqi, start = args
        s = jnp.einsum('btkgd,bskd->bkgts', qi, k, preferred_element_type=F32) * scale
        logits = s + (cqi[..., :, None] - ck[..., None, :])
        q_pos = start + jnp.arange(q_block)
        causal = k_pos[None, :] <= q_pos[:, None]
        p = jax.nn.softmax(jnp.where(causal, logits, -jnp.inf), axis=-1)
        return jnp.einsum('bkgts,bskd->btkgd', p.astype(v.dtype), v)

    out = lax.map(one_block, (qb, cqb, starts))
    return out.swapaxes(0, 1).reshape(b, tq, Q_DIM)


def moba_attention(q, k, v, q_offset, q_block):
    b, tq = q.shape[:2]
    L = k.shape[1]
    n_blk = max(-(-L // MOBA_BLOCK), MOBA_TOPK)
    pad = n_blk * MOBA_BLOCK - L

    def blockify(a):
        a = jnp.pad(a, ((0, 0), (0, pad), (0, 0), (0, 0)))
        return a.reshape(b, n_blk, MOBA_BLOCK, N_KV_HEADS, HEAD_DIM).transpose(0, 3, 1, 2, 4)

    kb, vb = blockify(k), blockify(v)
    k_mean = jnp.mean(kb.astype(F32), axis=3)
    slopes = jnp.exp2(-8.0 * jnp.arange(1, N_HEADS + 1, dtype=F32) / N_HEADS).reshape(N_KV_HEADS, GROUP)
    nbq = tq // q_block
    qb = q.reshape(b, nbq, q_block, N_KV_HEADS, GROUP, HEAD_DIM).swapaxes(0, 1)
    starts = q_offset + jnp.arange(nbq) * q_block
    b_idx = jnp.arange(b)[:, None, None, None, None]
    h_idx = jnp.arange(N_KV_HEADS)[None, None, :, None, None]
    blk_ids = jnp.arange(n_blk)
    in_blk = jnp.arange(MOBA_BLOCK)
    scale = HEAD_DIM ** -0.5

    def one_block(args):
        qi, start = args
        q_pos = start + jnp.arange(q_block)
        own = q_pos // MOBA_BLOCK
        gate = jnp.einsum('btkgd,bknd->btkgn', qi.astype(F32), k_mean)
        fully_past = blk_ids[None, :] < own[:, None]
        gate = jnp.where(fully_past[None, :, None, None, :], gate, -jnp.inf)
        _, top = lax.top_k(gate, MOBA_TOPK)
        own_b = jnp.broadcast_to(own[None, :, None, None, None], top.shape[:-1] + (1,)).astype(top.dtype)
        sel = jnp.concatenate([top, own_b], axis=-1)
        sel_ok = jnp.concatenate([top < own[None, :, None, None, None],
                                  jnp.ones(own_b.shape, dtype=bool)], axis=-1)
        kg = kb[b_idx, h_idx, sel]
        vg = vb[b_idx, h_idx, sel]
        s = jnp.einsum('btkgd,btkgnjd->btkgnj', qi, kg, preferred_element_type=F32) * scale
        dist = (q_pos[None, :, None, None, None, None]
                - (sel[..., None] * MOBA_BLOCK + in_blk)).astype(F32)
        ok = sel_ok[..., None] & (dist >= 0)
        logits = jnp.where(ok, s - slopes[None, None, :, :, None, None] * dist, -jnp.inf)
        p = jax.nn.softmax(logits.reshape(logits.shape[:4] + (-1,)), axis=-1).reshape(logits.shape)
        return jnp.einsum('btkgnj,btkgnjd->btkgd', p.astype(v.dtype), vg)

    out = lax.map(one_block, (qb, starts))
    return out.swapaxes(0, 1).reshape(b, tq, Q_DIM)


def sq_relu_mlp(x, g, w_up, w_down):
    h = rms_norm(x, g) @ w_up
    return jnp.square(jax.nn.relu(h)) @ w_down


def setup_inputs(seed: int = 0) -> dict:
    key = jax.random.key(seed)
    ks = jax.random.split(key, 20)
    n_pages = PAST_LEN // PAGE_SIZE
    n_used = DEC_BATCH * n_pages
    n_pool = n_used + n_used // 4
    nrm = jax.random.normal
    x_prompt = nrm(ks[0], (BATCH, SEQ, D_MODEL), F32)
    x_sample = nrm(ks[1], (DEC_BATCH, DEC_SEQ, D_MODEL), F32)
    cache_k = nrm(ks[2], (DEPTH, n_pool, PAGE_SIZE, N_KV_HEADS, HEAD_DIM), F32)
    cache_v = nrm(ks[3], (DEPTH, n_pool, PAGE_SIZE, N_KV_HEADS, HEAD_DIM), F32)
    cache_logf = jax.nn.log_sigmoid(FGATE_BIAS + nrm(ks[4], (N_FOX_LAYERS, n_pool, PAGE_SIZE, N_HEADS), F32))
    page_table = jax.random.permutation(ks[5], n_pool)[:n_used].reshape(DEC_BATCH, n_pages).astype(jnp.int32)
    attn_norm = 1.0 + 0.02 * nrm(ks[6], (DEPTH, D_MODEL), F32)
    w_qkv = nrm(ks[7], (DEPTH, D_MODEL, QKV_DIM), F32) * D_MODEL ** -0.5
    q_norm = 1.0 + 0.02 * nrm(ks[8], (DEPTH, HEAD_DIM), F32)
    k_norm = 1.0 + 0.02 * nrm(ks[9], (DEPTH, HEAD_DIM), F32)
    w_fgate = nrm(ks[10], (N_FOX_LAYERS, D_MODEL, N_HEADS), F32) * D_MODEL ** -0.5
    b_fgate = FGATE_BIAS + 0.1 * nrm(ks[11], (N_FOX_LAYERS, N_HEADS), F32)
    w_o = nrm(ks[12], (DEPTH, Q_DIM, D_MODEL), F32) * Q_DIM ** -0.5
    mlp_norm = 1.0 + 0.02 * nrm(ks[13], (DEPTH, D_MODEL), F32)
    w_up = nrm(ks[14], (DEPTH, D_MODEL, D_FF), F32) * D_MODEL ** -0.5
    w_down = nrm(ks[15], (DEPTH, D_FF, D_MODEL), F32) * D_FF ** -0.5
    return {"x_prompt": x_prompt, "x_sample": x_sample, "cache_k": cache_k, "cache_v": cache_v,
            "cache_logf": cache_logf, "page_table": page_table, "attn_norm": attn_norm, "w_qkv": w_qkv,
            "q_norm": q_norm, "k_norm": k_norm, "w_fgate": w_fgate, "b_fgate": b_fgate, "w_o": w_o,
            "mlp_norm": mlp_norm, "w_up": w_up, "w_down": w_down}


def reference(x_prompt, x_sample, cache_k, cache_v, cache_logf, page_table, attn_norm, w_qkv, q_norm, k_norm,
              w_fgate, b_fgate, w_o, mlp_norm, w_up, w_down):
    past_len = page_table.shape[1] * cache_k.shape[2]
    hp, hs = x_prompt, x_sample
    k_p, v_p, f_p, k_s, v_s, f_s = [], [], [], [], [], []
    for i in range(DEPTH):
        xp = rms_norm(hp, attn_norm[i])
        xs = rms_norm(hs, attn_norm[i])
        qp, kp, vp = project_qkv(xp, w_qkv[i], q_norm[i], k_norm[i])
        qs, ks_, vs = project_qkv(xs, w_qkv[i], q_norm[i], k_norm[i])
        k_all = jnp.concatenate([gather_past(cache_k[i], page_table).astype(ks_.dtype), ks_], axis=1)
        v_all = jnp.concatenate([gather_past(cache_v[i], page_table).astype(vs.dtype), vs], axis=1)
        k_p.append(kp); v_p.append(vp); k_s.append(ks_); v_s.append(vs)
        if i % N_MIXERS == 0:
            j = i // N_MIXERS
            lf_p = jax.nn.log_sigmoid((xp @ w_fgate[j]).astype(F32) + b_fgate[j].astype(F32))
            lf_s = jax.nn.log_sigmoid((xs @ w_fgate[j]).astype(F32) + b_fgate[j].astype(F32))
            c_p = jnp.cumsum(lf_p, axis=1)
            c_s = jnp.cumsum(jnp.concatenate([gather_past(cache_logf[j], page_table).astype(F32), lf_s], axis=1), axis=1)
            mix_p = fox_attention(qp, kp, vp, c_p, c_p, 0, FOX_Q_BLOCK)
            mix_s = fox_attention(qs, k_all, v_all, c_s[:, past_len:], c_s, past_len, qs.shape[1])
            f_p.append(lf_p.astype(cache_logf.dtype)); f_s.append(lf_s.astype(cache_logf.dtype))
        else:
            mix_p = moba_attention(qp, kp, vp, 0, MOBA_Q_BLOCK)
            mix_s = moba_attention(qs, k_all, v_all, past_len, 1)
        hp = hp + mix_p @ w_o[i]
        hs = hs + mix_s @ w_o[i]
        hp = hp + sq_relu_mlp(hp, mlp_norm[i], w_up[i], w_down[i])
        hs = hs + sq_relu_mlp(hs, mlp_norm[i], w_up[i], w_down[i])
    return (hp, hs, jnp.stack(k_p), jnp.stack(v_p), jnp.stack(f_p), jnp.stack(k_s), jnp.stack(v_s), jnp.stack(f_s))
```

```python
import functools
import math

import numpy as np
import jax
import jax.numpy as jnp
from jax import lax
from jax.experimental import pallas as pl
from jax.experimental.pallas import tpu as pltpu

F32 = jnp.float32
BF16 = jnp.bfloat16
LOG2E = 1.4426950408889634
RMS_EPS = 1e-6
MOBA_BLOCK = 256
MOBA_TOPK = 3
LANE = 128
ATTN_TILE = 512
PAGES_PER_STEP = 8
NEG_BIG = -(2.0 ** 100)
VMEM_LIMIT = 56 * 1024 * 1024
COL_BLK = 0
COL_QPOS = 64
COL_KPOS = 67


def _cparams(sem):
    return pltpu.CompilerParams(dimension_semantics=sem, vmem_limit_bytes=VMEM_LIMIT)


def _row_tile(rows, pref):
    t = min(rows, pref)
    while rows % t:
        t -= 8
    return t


def _split3(x):
    hi = x.astype(BF16).astype(F32)
    r1 = x - hi
    mid = r1.astype(BF16).astype(F32)
    lo = (r1 - mid).astype(BF16).astype(F32)
    return hi, mid, lo


def _dot_nt(a, b):
    return lax.dot_general(a, b, (((1,), (1,)), ((), ())), preferred_element_type=F32)


def _qkv_body(*refs, n_heads, n_kv, hd, q_scale, with_gate, q_f32):
    if with_gate:
        (x_ref, g_ref, w_ref, qg_ref, kg_ref, wf_ref, bf_ref,
         q_ref, k32_ref, v32_ref, kb_ref, vb_ref, lf_ref) = refs
    else:
        (x_ref, g_ref, w_ref, qg_ref, kg_ref,
         q_ref, k32_ref, v32_ref, kb_ref, vb_ref) = refs
    qd, kd = n_heads * hd, n_kv * hd
    x = x_ref[...]
    ms = jnp.mean(x * x, axis=-1, keepdims=True)
    xn = (x * lax.rsqrt(ms + RMS_EPS) * g_ref[...]).astype(BF16)

    def head_norm(a, gain):
        m = jnp.mean(a * a, axis=-1, keepdims=True)
        return a * lax.rsqrt(m + RMS_EPS) * gain

    chunk = 4 * hd
    for c in range(qd // chunk):
        acc = jnp.dot(xn, w_ref[:, c * chunk:(c + 1) * chunk], preferred_element_type=F32)
        for j in range(4):
            qn = head_norm(acc[:, j * hd:(j + 1) * hd], qg_ref[...])
            lo = c * chunk + j * hd
            if q_f32:
                q_ref[:, lo:lo + hd] = qn
            else:
                q_ref[:, lo:lo + hd] = (qn * q_scale).astype(BF16)
    acc = jnp.dot(xn, w_ref[:, qd:qd + kd], preferred_element_type=F32)
    for j in range(n_kv):
        kn = head_norm(acc[:, j * hd:(j + 1) * hd], kg_ref[...])
        k32_ref[:, j * hd:(j + 1) * hd] = kn
        kb_ref[:, j * hd:(j + 1) * hd] = kn.astype(BF16)
    acc = jnp.dot(xn, w_ref[:, qd + kd:qd + 2 * kd], preferred_element_type=F32)
    v32_ref[...] = acc
    vb_ref[...] = acc.astype(BF16)
    if with_gate:
        z = jnp.dot(xn, wf_ref[...], preferred_element_type=F32) + bf_ref[...]
        lf_ref[...] = jnp.minimum(z, 0.0) - jnp.log1p(jnp.exp(-jnp.abs(z)))


def _qkv_proj(x, g_attn, w_bf, q_gain, k_gain, wf_bf, bf_pad, *, n_heads, n_kv, hd, q_f32, tm):
    rows, d = x.shape
    qd, kd = n_heads * hd, n_kv * hd
    with_gate = wf_bf is not None
    tm = _row_tile(rows, tm)
    row_spec = lambda w: pl.BlockSpec((tm, w), lambda i: (i, 0))
    full = lambda a: pl.BlockSpec(a.shape, lambda i: (0,) * a.ndim)
    ins = [x, g_attn, w_bf, q_gain, k_gain]
    in_specs = [row_spec(d), full(g_attn), full(w_bf), full(q_gain), full(k_gain)]
    out_shape = [jax.ShapeDtypeStruct((rows, qd), F32 if q_f32 else BF16),
                 jax.ShapeDtypeStruct((rows, kd), F32), jax.ShapeDtypeStruct((rows, kd), F32),
                 jax.ShapeDtypeStruct((rows, kd), BF16), jax.ShapeDtypeStruct((rows, kd), BF16)]
    out_specs = [row_spec(qd), row_spec(kd), row_spec(kd), row_spec(kd), row_spec(kd)]
    if with_gate:
        ins += [wf_bf, bf_pad]
        in_specs += [full(wf_bf), full(bf_pad)]
        out_shape.append(jax.ShapeDtypeStruct((rows, LANE), F32))
        out_specs.append(row_spec(LANE))
    body = functools.partial(_qkv_body, n_heads=n_heads, n_kv=n_kv, hd=hd,
                             q_scale=hd ** -0.5 * LOG2E, with_gate=with_gate, q_f32=q_f32)
    return pl.pallas_call(body, grid=(rows // tm,), in_specs=in_specs, out_specs=out_specs,
                          out_shape=out_shape, compiler_params=_cparams(("arbitrary",)),
                          name="qkv_proj")(*ins)


def _fox_aug_body(lf_ref, qx_ref, kx_ref, carry_ref, *, n_heads, n_kv):
    t = pl.program_id(1)

    @pl.when(t == 0)
    def _():
        carry_ref[...] = jnp.zeros_like(carry_ref)

    lf = lf_ref[...]
    tc = lf.shape[0]
    tri = (lax.broadcasted_iota(jnp.int32, (tc, tc), 1)
           <= lax.broadcasted_iota(jnp.int32, (tc, tc), 0)).astype(F32)
    c = jnp.dot(tri, lf, preferred_element_type=F32, precision=lax.Precision.HIGHEST) + carry_ref[...]
    carry_ref[...] = c[tc - 1:tc, :]
    c2 = c * LOG2E
    lane = lax.broadcasted_iota(jnp.int32, (tc, LANE), 1)
    group = n_heads // n_kv
    ones3 = jnp.where(lane < 3, 1.0, 0.0)
    for kvh in range(n_kv):
        kaug = ones3
        for g in range(group):
            h = kvh * group + g
            hi, mid, lo = _split3(jnp.broadcast_to(c2[:, h:h + 1], (tc, LANE)))
            base = 3 + 3 * g
            own = jnp.where((lane >= base) & (lane < base + 3), 1.0, 0.0)
            qaug = jnp.where(lane == 0, hi, jnp.where(lane == 1, mid, jnp.where(lane == 2, lo, own)))
            qx_ref[:, h * LANE:(h + 1) * LANE] = qaug.astype(BF16)
            kaug = jnp.where(lane == base, -hi,
                             jnp.where(lane == base + 1, -mid, jnp.where(lane == base + 2, -lo, kaug)))
        kx_ref[:, kvh * LANE:(kvh + 1) * LANE] = kaug.astype(BF16)


def _fox_aug(lf128, *, batch, seq, n_heads, n_kv):
    tc = _row_tile(seq, 512)
    nt = seq // tc
    body = functools.partial(_fox_aug_body, n_heads=n_heads, n_kv=n_kv)
    return pl.pallas_call(
        body, grid=(batch, nt),
        in_specs=[pl.BlockSpec((tc, LANE), lambda b, t: (b * nt + t, 0))],
        out_specs=[pl.BlockSpec((tc, n_heads * LANE), lambda b, t: (b * nt + t, 0)),
                   pl.BlockSpec((tc, n_kv * LANE), lambda b, t: (b * nt + t, 0))],
        out_shape=[jax.ShapeDtypeStruct((batch * seq, n_heads * LANE), BF16),
                   jax.ShapeDtypeStruct((batch * seq, n_kv * LANE), BF16)],
        scratch_shapes=[pltpu.VMEM((1, LANE), F32)],
        compiler_params=_cparams(("arbitrary", "arbitrary")), name="fox_aug")(lf128)


def _kmean_body(k_ref, o_ref, *, nblk):
    k = k_ref[...]
    for j in range(nblk):
        o_ref[j:j + 1, :] = jnp.sum(k[j * MOBA_BLOCK:(j + 1) * MOBA_BLOCK, :], axis=0,
                                    keepdims=True) * (1.0 / MOBA_BLOCK)
    if o_ref.shape[0] > nblk:
        o_ref[nblk:, :] = jnp.zeros((o_ref.shape[0] - nblk, o_ref.shape[1]), F32)


def _kmean(k32, *, batch, seq):
    kd = k32.shape[1]
    nblk = seq // MOBA_BLOCK
    assert nblk <= COL_QPOS
    return pl.pallas_call(
        functools.partial(_kmean_body, nblk=nblk), grid=(batch,),
        in_specs=[pl.BlockSpec((seq, kd), lambda b: (b, 0))],
        out_specs=pl.BlockSpec((None, LANE, kd), lambda b: (b, 0, 0)),
        out_shape=jax.ShapeDtypeStruct((batch, LANE, kd), F32),
        compiler_params=_cparams(("arbitrary",)), name="moba_kmean")(k32)


def _top3_allowed(gate, lane, own):
    lane_f = lane.astype(F32)
    gm = jnp.where(lane < own, gate, -jnp.inf)
    sel = jnp.zeros(gate.shape, F32)
    for _ in range(MOBA_TOPK):
        mx = jnp.max(gm, axis=1, keepdims=True)
        idx = jnp.min(jnp.where(gm == mx, lane_f, float(LANE)), axis=1, keepdims=True)
        pick = lane_f == idx
        sel = jnp.where(pick, 1.0, sel)
        gm = jnp.where(pick, -jnp.inf, gm)
    return jnp.where(lane < own, sel, jnp.where(lane == own, 1.0, 0.0))


def _flash_body(qi_ref, kj_ref, last_ref, *refs, group, hd, moba, tile):
    if moba:
        q_ref, qx_ref, k_ref, kx_ref, v_ref, kmt_ref, o_ref, qaug, m_sc, l_sc, acc_sc = refs
    else:
        q_ref, qx_ref, k_ref, kx_ref, v_ref, o_ref, qaug, m_sc, l_sc, acc_sc = refs
    p = pl.program_id(2)
    i, j = qi_ref[p], kj_ref[p]

    @pl.when(j == i)
    def _():
        for g in range(group):
            if moba:
                qf = q_ref[:, g * hd:(g + 1) * hd]
                gate = jnp.dot(qf, kmt_ref[...], preferred_element_type=F32,
                               precision=lax.Precision.HIGHEST)
                lane = lax.broadcasted_iota(jnp.int32, (tile, LANE), 1)
                pos = i * tile + lax.broadcasted_iota(jnp.int32, (tile, LANE), 0)
                own = lax.shift_right_logical(pos, int(math.log2(MOBA_BLOCK)))
                allowed = _top3_allowed(gate, lane, own)
                bias = jnp.where(allowed > 0.0, 0.0, NEG_BIG)
                aug = jnp.where(lane < COL_QPOS, bias, qx_ref[:, g * LANE:(g + 1) * LANE].astype(F32))
                qaug[g, :, :hd] = (qf * (hd ** -0.5 * LOG2E)).astype(BF16)
                qaug[g, :, hd:] = aug.astype(BF16)
            else:
                qaug[g, :, :hd] = q_ref[:, g * hd:(g + 1) * hd]
                qaug[g, :, hd:] = qx_ref[:, g * LANE:(g + 1) * LANE]
        m_sc[...] = jnp.full(m_sc.shape, -jnp.inf, F32)
        l_sc[...] = jnp.zeros(l_sc.shape, F32)
        acc_sc[...] = jnp.zeros(acc_sc.shape, F32)

    def step(diagonal):
        kaug = jnp.concatenate([k_ref[...], kx_ref[...]], axis=1)
        v = v_ref[...]
        for g in range(group):
            s = _dot_nt(qaug[g], kaug)
            if diagonal:
                causal = (lax.broadcasted_iota(jnp.int32, (tile, tile), 1)
                          <= lax.broadcasted_iota(jnp.int32, (tile, tile), 0))
                s = jnp.where(causal, s, -jnp.inf)
            m_prev = m_sc[g]
            m_new = jnp.maximum(m_prev, jnp.max(s, axis=1, keepdims=True))
            alpha = jnp.exp2(m_prev - m_new)
            pr = jnp.exp2(s - m_new)
            l_sc[g] = alpha * l_sc[g] + jnp.sum(pr, axis=1, keepdims=True)
            acc_sc[g] = alpha * acc_sc[g] + jnp.dot(pr.astype(BF16), v, preferred_element_type=F32)
            m_sc[g] = m_new

    pl.when(j == i)(lambda: step(True))
    pl.when(j != i)(lambda: step(False))

    @pl.when(last_ref[p] == 1)
    def _():
        for g in range(group):
            o_ref[:, g * hd:(g + 1) * hd] = (acc_sc[g] / l_sc[g]).astype(o_ref.dtype)


def _flash_prompt(q, qx, kb, kx, vb, kmt, *, batch, seq, n_heads, n_kv, hd, moba):
    tile = ATTN_TILE
    assert seq % tile == 0 and tile % MOBA_BLOCK == 0
    nt = seq // tile
    group = n_heads // n_kv
    qi = np.concatenate([np.full(i + 1, i) for i in range(nt)]).astype(np.int32)
    kj = np.concatenate([np.array([i] + list(range(i))) for i in range(nt)]).astype(np.int32)
    last = np.concatenate([np.array([0] * i + [1]) for i in range(nt)]).astype(np.int32)
    gw = group * hd
    q_map = lambda b, h, p, qi, kj, la: (b * nt + qi[p], h)
    k_map = lambda b, h, p, qi, kj, la: (b * nt + kj[p], h)
    in_specs = [pl.BlockSpec((tile, gw), q_map)]
    if moba:
        in_specs += [pl.BlockSpec((tile, group * LANE), lambda b, h, p, qi, kj, la: (qi[p], h)),
                     pl.BlockSpec((tile, hd), k_map),
                     pl.BlockSpec((tile, LANE), lambda b, h, p, qi, kj, la: (kj[p], 0)),
                     pl.BlockSpec((tile, hd), k_map),
                     pl.BlockSpec((None, None, hd, LANE), lambda b, h, p, qi, kj, la: (b, h, 0, 0))]
        ins = [q, qx, kb, kx, vb, kmt]
    else:
        in_specs += [pl.BlockSpec((tile, group * LANE), q_map),
                     pl.BlockSpec((tile, hd), k_map),
                     pl.BlockSpec((tile, LANE), k_map),
                     pl.BlockSpec((tile, hd), k_map)]
        ins = [q, qx, kb, kx, vb]
    body = functools.partial(_flash_body, group=group, hd=hd, moba=moba, tile=tile)
    grid_spec = pltpu.PrefetchScalarGridSpec(
        num_scalar_prefetch=3, grid=(batch, n_kv, len(qi)), in_specs=in_specs,
        out_specs=pl.BlockSpec((tile, gw), q_map),
        scratch_shapes=[pltpu.VMEM((group, tile, hd + LANE), BF16),
                        pltpu.VMEM((group, tile, 1), F32), pltpu.VMEM((group, tile, 1), F32),
                        pltpu.VMEM((group, tile, hd), F32)])
    return pl.pallas_call(
        body, grid_spec=grid_spec, out_shape=jax.ShapeDtypeStruct((batch * seq, n_heads * hd), BF16),
        compiler_params=_cparams(("arbitrary", "arbitrary", "arbitrary")),
        name="moba_prompt" if moba else "fox_prompt")(jnp.asarray(qi), jnp.asarray(kj),
                                                      jnp.asarray(last), *ins)


def _moba_static_cols(seq, n_heads):
    pos = jnp.arange(seq, dtype=jnp.int32)
    slopes = jnp.exp2(-8.0 * jnp.arange(1, n_heads + 1, dtype=F32) / n_heads) * LOG2E
    qparts = _split3(-(slopes[None, :] * pos.astype(F32)[:, None]))
    sparts = _split3(slopes)
    qx = jnp.zeros((seq, n_heads, LANE), F32)
    for c in range(3):
        qx = qx.at[:, :, COL_QPOS + c].set(qparts[c])
        qx = qx.at[:, :, COL_KPOS + c].set(jnp.broadcast_to(sparts[c][None, :], (seq, n_heads)))
        qx = qx.at[:, :, COL_KPOS + 3 + c].set(jnp.broadcast_to(sparts[c][None, :], (seq, n_heads)))
    blk = pos // MOBA_BLOCK
    lane = jnp.arange(LANE, dtype=jnp.int32)[None, :]
    kx = jnp.where(lane == blk[:, None], 1.0, 0.0)
    kx = jnp.where((lane >= COL_QPOS) & (lane < COL_QPOS + 3), 1.0, kx)
    kx = jnp.where((lane >= COL_KPOS) & (lane < COL_KPOS + 3), (blk * MOBA_BLOCK).astype(F32)[:, None], kx)
    kx = jnp.where((lane >= COL_KPOS + 3) & (lane < COL_KPOS + 6), (pos % MOBA_BLOCK).astype(F32)[:, None], kx)
    return qx.reshape(seq, n_heads * LANE).astype(BF16), kx.astype(BF16)


def _page_specs(n, block, layer_arg, n_pages, pages_per_step):
    def spec(i):
        def imap(b, s, pt):
            return (layer_arg, pt[b * n_pages + s * pages_per_step + i]) + (0,) * (len(block) - 2)
        return pl.BlockSpec(block, imap)
    return [spec(i) for i in range(n)]


def _cum_sample_body(pt_ref, *refs, pg):
    lf_refs, new_ref, c_ref, cnew_ref, carry_ref = refs[:pg], refs[pg], refs[pg + 1], refs[pg + 2], refs[pg + 3]
    s = pl.program_id(1)

    @pl.when(s == 0)
    def _():
        carry_ref[...] = jnp.zeros_like(carry_ref)

    triu = (lax.broadcasted_iota(jnp.int32, (LANE, LANE), 0)
            <= lax.broadcasted_iota(jnp.int32, (LANE, LANE), 1)).astype(F32)
    carry = carry_ref[...]
    for i in range(pg):
        c = jnp.dot(lf_refs[i][...], triu, preferred_element_type=F32,
                    precision=lax.Precision.HIGHEST) + carry
        c_ref[:, i * LANE:(i + 1) * LANE] = c
        carry = jnp.broadcast_to(c[:, LANE - 1:LANE], c.shape)
    carry_ref[...] = carry

    @pl.when(s == pl.num_programs(1) - 1)
    def _():
        cnew_ref[...] = jnp.dot(new_ref[...], triu, preferred_element_type=F32,
                                precision=lax.Precision.HIGHEST) + carry


def _cum_sample(page_table, logf_t, lfnew_t, *, n_heads):
    db, n_pages = page_table.shape
    page = logf_t.shape[2]
    assert page == LANE
    pg = math.gcd(PAGES_PER_STEP, n_pages)
    in_specs = _page_specs(pg, (None, None, n_heads, page), 0, n_pages, pg)
    in_specs.append(pl.BlockSpec((None, n_heads, LANE), lambda b, s, pt: (b, 0, 0)))
    grid_spec = pltpu.PrefetchScalarGridSpec(
        num_scalar_prefetch=1, grid=(db, n_pages // pg), in_specs=in_specs,
        out_specs=[pl.BlockSpec((None, n_heads, pg * page), lambda b, s, pt: (b, 0, s)),
                   pl.BlockSpec((None, n_heads, LANE), lambda b, s, pt: (b, 0, 0))],
        scratch_shapes=[pltpu.VMEM((n_heads, LANE), F32)])
    return pl.pallas_call(
        functools.partial(_cum_sample_body, pg=pg), grid_spec=grid_spec,
        out_shape=[jax.ShapeDtypeStruct((db, n_heads, n_pages * page), F32),
                   jax.ShapeDtypeStruct((db, n_heads, LANE), F32)],
        compiler_params=_cparams(("arbitrary", "arbitrary")),
        name="fox_sample_cumsum")(page_table.reshape(-1), *([logf_t[None]] * pg), lfnew_t)


def _row_token(shape, n_heads, dec_seq):
    row = lax.broadcasted_iota(jnp.int32, shape, 0)
    tok = jnp.zeros(shape, jnp.int32)
    for t in range(1, dec_seq):
        tok = tok + (row >= t * n_heads).astype(jnp.int32)
    return tok


def _stage_pages(k_refs, v_refs, kb, vb, page):
    for i, (k_ref, v_ref) in enumerate(zip(k_refs, v_refs)):
        kb[i * page:(i + 1) * page, :] = k_ref[...].astype(BF16)
        vb[i * page:(i + 1) * page, :] = v_ref[...].astype(BF16)


def _fox_sample_body(pt_ref, *refs, pg, page, dec_seq, n_heads):
    k_refs, v_refs = refs[:pg], refs[pg:2 * pg]
    (q_ref, cq_ref, c_ref, cnew_ref, kn_ref, vn_ref, o_ref, kb, vb, m_sc, l_sc, acc_sc) = refs[2 * pg:]
    s = pl.program_id(1)

    @pl.when(s == 0)
    def _():
        m_sc[...] = jnp.full(m_sc.shape, -jnp.inf, F32)
        l_sc[...] = jnp.zeros(l_sc.shape, F32)
        acc_sc[...] = jnp.zeros(acc_sc.shape, F32)

    def update(logits, v):
        m_prev = m_sc[...]
        m_new = jnp.maximum(m_prev, jnp.max(logits, axis=1, keepdims=True))
        alpha = jnp.exp2(m_prev - m_new)
        pr = jnp.exp2(logits - m_new)
        l_sc[...] = alpha * l_sc[...] + jnp.sum(pr, axis=1, keepdims=True)
        acc_sc[...] = alpha * acc_sc[...] + jnp.dot(pr.astype(BF16), v, preferred_element_type=F32)
        m_sc[...] = m_new

    q = q_ref[...]
    cq = cq_ref[...]
    _stage_pages(k_refs, v_refs, kb, vb, page)
    ck = jnp.concatenate([c_ref[...]] * dec_seq, axis=0)
    update(_dot_nt(q, kb[...]) + (cq - ck) * LOG2E, vb[...])

    @pl.when(s == pl.num_programs(1) - 1)
    def _():
        rows = q.shape[0]
        ckn = jnp.concatenate([cnew_ref[...]] * dec_seq, axis=0)
        logits = _dot_nt(q, kn_ref[...].astype(BF16)) + (cq - ckn) * LOG2E
        tok = _row_token((rows, LANE), n_heads, dec_seq)
        logits = jnp.where(lax.broadcasted_iota(jnp.int32, (rows, LANE), 1) <= tok, logits, -jnp.inf)
        update(logits, vn_ref[...].astype(BF16))
        o_ref[...] = acc_sc[...] / l_sc[...]


def _moba_sample_body(pt_ref, *refs, pg, page, dec_seq, n_heads, past):
    k_refs, v_refs = refs[:pg], refs[pg:2 * pg]
    (q_ref, q32_ref, slope_ref, kn_ref, vn_ref, o_ref,
     kb, vb, acc_all, m_all, l_all, km) = refs[2 * pg:]
    s = pl.program_id(1)
    nblk = past // MOBA_BLOCK
    ppb = MOBA_BLOCK // page
    rows = q_ref.shape[0]
    lane = lax.broadcasted_iota(jnp.int32, (rows, LANE), 1)
    qpos = (past + _row_token((rows, 1), n_heads, dec_seq)).astype(F32)
    km_row = lax.broadcasted_iota(jnp.int32, km.shape, 0)

    @pl.when(s == 0)
    def _():
        m_all[...] = jnp.full(m_all.shape, -jnp.inf, F32)
        l_all[...] = jnp.zeros(l_all.shape, F32)
        km[...] = jnp.zeros(km.shape, F32)

    def block_partial(logits, v, blk):
        mj = jnp.max(logits, axis=1, keepdims=True)
        pr = jnp.exp2(logits - mj)
        acc_all[blk] = jnp.dot(pr.astype(BF16), v, preferred_element_type=F32)
        m_all[...] = jnp.where(lane == blk, mj, m_all[...])
        l_all[...] = jnp.where(lane == blk, jnp.sum(pr, axis=1, keepdims=True), l_all[...])

    q = q_ref[...]
    slope = slope_ref[...]
    _stage_pages(k_refs, v_refs, kb, vb, page)
    width = pg * page
    kpos = (s * width + lax.broadcasted_iota(jnp.int32, (1, width), 1)).astype(F32)
    logits = _dot_nt(q, kb[...]) - slope * (qpos - kpos)
    for jb in range(pg // ppb):
        blk = s * (pg // ppb) + jb
        block_partial(logits[:, jb * MOBA_BLOCK:(jb + 1) * MOBA_BLOCK],
                      vb[jb * MOBA_BLOCK:(jb + 1) * MOBA_BLOCK, :], blk)
        ksum = k_refs[jb * ppb][...].sum(axis=0, keepdims=True)
        for i in range(1, ppb):
            ksum = ksum + k_refs[jb * ppb + i][...].sum(axis=0, keepdims=True)
        km[...] = jnp.where(km_row == blk, ksum * (1.0 / MOBA_BLOCK), km[...])

    @pl.when(s == pl.num_programs(1) - 1)
    def _():
        npos = (past + lax.broadcasted_iota(jnp.int32, (1, LANE), 1)).astype(F32)
        ln = _dot_nt(q, kn_ref[...].astype(BF16)) - slope * (qpos - npos)
        ln = jnp.where(npos <= qpos, ln, -jnp.inf)
        block_partial(ln, vn_ref[...].astype(BF16), nblk)
        gate = lax.dot_general(q32_ref[...], km[...], (((1,), (1,)), ((), ())),
                               preferred_element_type=F32, precision=lax.Precision.HIGHEST)
        allowed = _top3_allowed(gate, lane, jnp.full((rows, LANE), nblk, jnp.int32)) > 0.0
        m_row = jnp.max(jnp.where(allowed, m_all[...], -jnp.inf), axis=1, keepdims=True)
        w = jnp.where(allowed, jnp.exp2(m_all[...] - m_row), 0.0)
        denom = jnp.sum(w * l_all[...], axis=1, keepdims=True)
        out = jnp.zeros(o_ref.shape, F32)
        for jb in range(nblk + 1):
            out = out + w[:, jb:jb + 1] * acc_all[jb]
        o_ref[...] = out / denom


def _sample_attention(page_table, cache_k4, cache_v4, layer, qbd, kn_pad, vn_pad, extra, *, moba,
                      dec_seq, n_heads):
    db, n_pages = page_table.shape
    page, kd = cache_k4.shape[2], cache_k4.shape[3]
    rows = qbd.shape[1]
    past = n_pages * page
    pg = math.gcd(PAGES_PER_STEP, n_pages)
    assert page == LANE and MOBA_BLOCK % page == 0 and pg % (MOBA_BLOCK // page) == 0
    assert past % MOBA_BLOCK == 0 and past // MOBA_BLOCK < LANE
    page_specs = (_page_specs(pg, (None, None, page, kd), layer, n_pages, pg)
                  + _page_specs(pg, (None, None, page, kd), layer, n_pages, pg))
    per_b = lambda a: pl.BlockSpec((None,) + a.shape[1:], lambda b, s, pt: (b,) + (0,) * (a.ndim - 1))
    common = [pltpu.VMEM((pg * page, kd), BF16), pltpu.VMEM((pg * page, kd), BF16)]
    if moba:
        q32, slope = extra
        ins = [qbd, q32, slope, kn_pad, vn_pad]
        in_specs = page_specs + [per_b(qbd), per_b(q32),
                                 pl.BlockSpec(slope.shape, lambda b, s, pt: (0, 0)),
                                 per_b(kn_pad), per_b(vn_pad)]
        scratch = common + [pltpu.VMEM((past // MOBA_BLOCK + 1, rows, kd), F32),
                            pltpu.VMEM((rows, LANE), F32), pltpu.VMEM((rows, LANE), F32),
                            pltpu.VMEM((LANE, kd), F32)]
        body = functools.partial(_moba_sample_body, pg=pg, page=page, dec_seq=dec_seq,
                                 n_heads=n_heads, past=past)
    else:
        cq, c_t, cnew_t = extra
        ins = [qbd, cq, c_t, cnew_t, kn_pad, vn_pad]
        in_specs = page_specs + [per_b(qbd), per_b(cq),
                                 pl.BlockSpec((None, n_heads, pg * page), lambda b, s, pt: (b, 0, s)),
                                 per_b(cnew_t), per_b(kn_pad), per_b(vn_pad)]
        scratch = common + [pltpu.VMEM((rows, 1), F32), pltpu.VMEM((rows, 1), F32),
                            pltpu.VMEM((rows, kd), F32)]
        body = functools.partial(_fox_sample_body, pg=pg, page=page, dec_seq=dec_seq, n_heads=n_heads)
    grid_spec = pltpu.PrefetchScalarGridSpec(
        num_scalar_prefetch=1, grid=(db, n_pages // pg), in_specs=in_specs,
        out_specs=pl.BlockSpec((None, rows, kd), lambda b, s, pt: (b, 0, 0)), scratch_shapes=scratch)
    return pl.pallas_call(
        body, grid_spec=grid_spec, out_shape=jax.ShapeDtypeStruct((db, rows, kd), F32),
        compiler_params=_cparams(("arbitrary", "arbitrary")),
        name="moba_sample" if moba else "fox_sample")(
            page_table.reshape(-1), *([cache_k4] * pg), *([cache_v4] * pg), *ins)


def _wo_body(x_ref, mix_ref, w_ref, o_ref):
    o_ref[...] = x_ref[...] + jnp.dot(mix_ref[...], w_ref[...], preferred_element_type=F32)


def _wo_residual(x, mix_bf, wo_bf, *, tm):
    rows, d = x.shape
    tm = _row_tile(rows, tm)
    return pl.pallas_call(
        _wo_body, grid=(rows // tm,),
        in_specs=[pl.BlockSpec((tm, d), lambda i: (i, 0)),
                  pl.BlockSpec((tm, mix_bf.shape[1]), lambda i: (i, 0)),
                  pl.BlockSpec(wo_bf.shape, lambda i: (0, 0))],
        out_specs=pl.BlockSpec((tm, d), lambda i: (i, 0)),
        out_shape=jax.ShapeDtypeStruct((rows, d), F32),
        compiler_params=_cparams(("arbitrary",)), name="wo_residual")(x, mix_bf, wo_bf)


def _mlp_body(h_ref, g_ref, wu_ref, wd_ref, o_ref, hn_sc):
    f = pl.program_id(1)

    @pl.when(f == 0)
    def _():
        h = h_ref[...]
        ms = jnp.mean(h * h, axis=-1, keepdims=True)
        hn_sc[...] = (h * lax.rsqrt(ms + RMS_EPS) * g_ref[...]).astype(BF16)
        o_ref[...] = h

    u = jnp.maximum(jnp.dot(hn_sc[...], wu_ref[...], preferred_element_type=F32), 0.0)
    o_ref[...] += jnp.dot((u * u).astype(BF16), wd_ref[...], preferred_element_type=F32)


def _mlp_residual(h, g_mlp, wu_bf, wd_bf, *, tm, tf):
    rows, d = h.shape
    dff = wu_bf.shape[1]
    tm = _row_tile(rows, tm)
    tf = min(tf, dff)
    assert dff % tf == 0
    return pl.pallas_call(
        _mlp_body, grid=(rows // tm, dff // tf),
        in_specs=[pl.BlockSpec((tm, d), lambda i, f: (i, 0)),
                  pl.BlockSpec(g_mlp.shape, lambda i, f: (0, 0)),
                  pl.BlockSpec((d, tf), lambda i, f: (0, f)),
                  pl.BlockSpec((tf, d), lambda i, f: (f, 0))],
        out_specs=pl.BlockSpec((tm, d), lambda i, f: (i, 0)),
        out_shape=jax.ShapeDtypeStruct((rows, d), F32),
        scratch_shapes=[pltpu.VMEM((tm, d), BF16)],
        compiler_params=_cparams(("arbitrary", "arbitrary")), name="mlp_residual")(h, g_mlp, wu_bf, wd_bf)


def kernel(x_prompt, x_sample, cache_k, cache_v, cache_logf, page_table, attn_norm, w_qkv, q_norm, k_norm,
           w_fgate, b_fgate, w_o, mlp_norm, w_up, w_down):
    batch, seq, d = x_prompt.shape
    db, dec_seq, _ = x_sample.shape
    depth, n_pool, page, n_kv, hd = cache_k.shape
    n_heads = w_fgate.shape[2]
    group = n_heads // n_kv
    qd, kd = n_heads * hd, n_kv * hd
    n_pages = page_table.shape[1]
    past = n_pages * page
    rows_s = dec_seq * n_heads

    hp = x_prompt.reshape(batch * seq, d)
    hs = x_sample.reshape(db * dec_seq, d)
    cache_k4 = cache_k.reshape(depth, n_pool, page, kd)
    cache_v4 = cache_v.reshape(depth, n_pool, page, kd)
    row_kv = (jnp.arange(rows_s) % n_heads) // group
    row_mask = (row_kv[:, None] == jnp.arange(n_kv)[None, :]).astype(F32)
    slopes = jnp.exp2(-8.0 * jnp.arange(1, n_heads + 1, dtype=F32) / n_heads)

    def block_diag(q2d, dtype):
        q = q2d.astype(F32).reshape(db, rows_s, 1, hd) * row_mask[None, :, :, None]
        return q.reshape(db, rows_s, kd).astype(dtype)

    def pad_page(a2d):
        a = a2d.reshape(db, dec_seq, kd)
        return jnp.pad(a, ((0, 0), (0, LANE - dec_seq), (0, 0)))

    def own_lanes(o):
        o6 = o.reshape(db, dec_seq, n_kv, group, n_kv, hd)
        return jnp.stack([o6[:, :, k, :, k, :] for k in range(n_kv)], axis=2).reshape(db * dec_seq, qd)

    outs = {k: [] for k in ("kp", "vp", "fp", "ks", "vs", "fs")}
    for i in range(depth):
        fox = i % 2 == 0
        j = i // 2
        w_bf = w_qkv[i].astype(BF16)
        g_attn, qg, kg = attn_norm[i][None, :], q_norm[i][None, :], k_norm[i][None, :]
        if fox:
            wf_bf = jnp.pad(w_fgate[j], ((0, 0), (0, LANE - n_heads))).astype(BF16)
            bf_pad = jnp.pad(b_fgate[j], (0, LANE - n_heads))[None, :]
        else:
            wf_bf = bf_pad = None
        proj = functools.partial(_qkv_proj, g_attn=g_attn, w_bf=w_bf, q_gain=qg, k_gain=kg, wf_bf=wf_bf,
                                 bf_pad=bf_pad, n_heads=n_heads, n_kv=n_kv, hd=hd, q_f32=not fox)
        res_p = proj(hp, tm=512)
        res_s = proj(hs, tm=128)
        qp, kp32, vp32, kpb, vpb = res_p[:5]
        qs, ks32, vs32 = res_s[:3]
        outs["kp"].append(kp32.reshape(batch, seq, n_kv, hd))
        outs["vp"].append(vp32.reshape(batch, seq, n_kv, hd))
        outs["ks"].append(ks32.reshape(db, dec_seq, n_kv, hd))
        outs["vs"].append(vs32.reshape(db, dec_seq, n_kv, hd))
        kn_pad, vn_pad = pad_page(ks32), pad_page(vs32)
        if fox:
            lfp, lfs = res_p[5], res_s[5]
            outs["fp"].append(lfp[:, :n_heads].reshape(batch, seq, n_heads))
            outs["fs"].append(lfs[:, :n_heads].reshape(db, dec_seq, n_heads))
            qx, kx = _fox_aug(lfp, batch=batch, seq=seq, n_heads=n_heads, n_kv=n_kv)
            mix_p = _flash_prompt(qp, qx, kpb, kx, vpb, None, batch=batch, seq=seq, n_heads=n_heads,
                                  n_kv=n_kv, hd=hd, moba=False)
            logf_t = jnp.swapaxes(cache_logf[j], 1, 2)
            lfnew_t = jnp.pad(jnp.swapaxes(lfs[:, :n_heads].reshape(db, dec_seq, n_heads), 1, 2),
                              ((0, 0), (0, 0), (0, LANE - dec_seq)))
            c_t, cnew_t = _cum_sample(page_table, logf_t, lfnew_t, n_heads=n_heads)
            cq = jnp.swapaxes(cnew_t[:, :, :dec_seq], 1, 2).reshape(db, rows_s, 1)
            o_s = _sample_attention(page_table, cache_k4, cache_v4, i, block_diag(qs, BF16), kn_pad, vn_pad,
                                    (cq, c_t, cnew_t), moba=False, dec_seq=dec_seq, n_heads=n_heads)
        else:
            kmean = _kmean(kp32, batch=batch, seq=seq)
            kmt = jnp.swapaxes(kmean.reshape(batch, LANE, n_kv, hd), 1, 3)
            kmt = jnp.swapaxes(kmt, 1, 2)
            qx, kx = _moba_static_cols(seq, n_heads)
            mix_p = _flash_prompt(qp, qx, kpb, kx, vpb, kmt, batch=batch, seq=seq, n_heads=n_heads,
                                  n_kv=n_kv, hd=hd, moba=True)
            slope_col = (jnp.tile(slopes, dec_seq) * LOG2E)[:, None]
            q_scaled = qs * (hd ** -0.5 * LOG2E)
            o_s = _sample_attention(page_table, cache_k4, cache_v4, i, block_diag(q_scaled, BF16), kn_pad,
                                    vn_pad, (block_diag(qs, F32), slope_col), moba=True, dec_seq=dec_seq,
                                    n_heads=n_heads)
        mix_s = own_lanes(o_s).astype(BF16)
        wo_bf = w_o[i].astype(BF16)
        hp = _wo_residual(hp, mix_p, wo_bf, tm=512)
        hs = _wo_residual(hs, mix_s, wo_bf, tm=128)
        wu_bf, wd_bf = w_up[i].astype(BF16), w_down[i].astype(BF16)
        g_mlp = mlp_norm[i][None, :]
        hp = _mlp_residual(hp, g_mlp, wu_bf, wd_bf, tm=512, tf=512)
        hs = _mlp_residual(hs, g_mlp, wu_bf, wd_bf, tm=128, tf=512)
    return (hp.reshape(batch, seq, d), hs.reshape(db, dec_seq, d),
            jnp.stack(outs["kp"]), jnp.stack(outs["vp"]), jnp.stack(outs["fp"]),
            jnp.stack(outs["ks"]), jnp.stack(outs["vs"]), jnp.stack(outs["fs"]))
```

```python
import functools
import math

import numpy as np
import jax
import jax.numpy as jnp
from jax import lax
from jax.experimental import pallas as pl
from jax.experimental.pallas import tpu as pltpu

F32 = jnp.float32
BF16 = jnp.bfloat16
LOG2E = 1.4426950408889634
RMS_EPS = 1e-6
MOBA_BLOCK = 256
MOBA_TOPK = 3
LANE = 128
ATTN_TILE = 512
FLASH_CHUNK = 512
PAGES_PER_STEP = 8
NEG_BIG = -(2.0 ** 100)
VMEM_LIMIT = 56 * 1024 * 1024
COL_BLK = 0
COL_QPOS = 64
COL_KPOS = 67


def _cparams(sem):
    return pltpu.CompilerParams(dimension_semantics=sem, vmem_limit_bytes=VMEM_LIMIT)


def _row_tile(rows, pref):
    t = min(rows, pref)
    while rows % t:
        t -= 8
    return t


def _split3(x):
    hi = x.astype(BF16).astype(F32)
    r1 = x - hi
    mid = r1.astype(BF16).astype(F32)
    lo = (r1 - mid).astype(BF16).astype(F32)
    return hi, mid, lo


def _dot_nt(a, b):
    return lax.dot_general(a, b, (((1,), (1,)), ((), ())), preferred_element_type=F32)


def _qkv_body(*refs, n_heads, n_kv, hd, q_scale, with_gate, q_f32, attn_copies):
    n_in = 7 if with_gate else 5
    x_ref, g_ref, w_ref, qg_ref, kg_ref = refs[:5]
    outs = list(refs[n_in:])
    q_ref, k32_ref, v32_ref = outs[:3]
    del outs[:3]
    if attn_copies:
        kb_ref, vt_ref = outs[:2]
        del outs[:2]
    qd, kd = n_heads * hd, n_kv * hd
    x = x_ref[...]
    ms = jnp.mean(x * x, axis=-1, keepdims=True)
    xn = (x * lax.rsqrt(ms + RMS_EPS) * g_ref[...]).astype(BF16)

    def head_norm(a, gain):
        m = jnp.mean(a * a, axis=-1, keepdims=True)
        return a * lax.rsqrt(m + RMS_EPS) * gain

    chunk = 4 * hd
    for c in range(qd // chunk):
        acc = jnp.dot(xn, w_ref[:, c * chunk:(c + 1) * chunk], preferred_element_type=F32)
        for j in range(4):
            qn = head_norm(acc[:, j * hd:(j + 1) * hd], qg_ref[...])
            lo = c * chunk + j * hd
            if q_f32:
                q_ref[:, lo:lo + hd] = qn
            else:
                q_ref[:, lo:lo + hd] = (qn * q_scale).astype(BF16)
    acc = jnp.dot(xn, w_ref[:, qd:qd + kd], preferred_element_type=F32)
    for j in range(n_kv):
        kn = head_norm(acc[:, j * hd:(j + 1) * hd], kg_ref[...])
        k32_ref[:, j * hd:(j + 1) * hd] = kn
        if attn_copies:
            kb_ref[:, j * hd:(j + 1) * hd] = kn.astype(BF16)
    acc = jnp.dot(xn, w_ref[:, qd + kd:qd + 2 * kd], preferred_element_type=F32)
    v32_ref[...] = acc
    if attn_copies:
        vt_ref[...] = acc.T.astype(BF16)
    if with_gate:
        wf_ref, bf_ref = refs[5:7]
        z = jnp.dot(xn, wf_ref[...], preferred_element_type=F32) + bf_ref[...]
        outs[0][...] = jnp.minimum(z, 0.0) - jnp.log1p(jnp.exp(-jnp.abs(z)))


def _qkv_proj(x, g_attn, w_bf, q_gain, k_gain, wf_bf, bf_pad, *, n_heads, n_kv, hd, q_f32, attn_copies, tm):
    rows, d = x.shape
    qd, kd = n_heads * hd, n_kv * hd
    with_gate = wf_bf is not None
    tm = _row_tile(rows, tm)
    row_spec = lambda w: pl.BlockSpec((tm, w), lambda i: (i, 0))
    full = lambda a: pl.BlockSpec(a.shape, lambda i: (0,) * a.ndim)
    ins = [x, g_attn, w_bf, q_gain, k_gain]
    in_specs = [row_spec(d), full(g_attn), full(w_bf), full(q_gain), full(k_gain)]
    out_shape = [jax.ShapeDtypeStruct((rows, qd), F32 if q_f32 else BF16),
                 jax.ShapeDtypeStruct((rows, kd), F32), jax.ShapeDtypeStruct((rows, kd), F32)]
    out_specs = [row_spec(qd), row_spec(kd), row_spec(kd)]
    if attn_copies:
        out_shape += [jax.ShapeDtypeStruct((rows, kd), BF16), jax.ShapeDtypeStruct((kd, rows), BF16)]
        out_specs += [row_spec(kd), pl.BlockSpec((kd, tm), lambda i: (0, i))]
    if with_gate:
        ins += [wf_bf, bf_pad]
        in_specs += [full(wf_bf), full(bf_pad)]
        out_shape.append(jax.ShapeDtypeStruct((rows, LANE), F32))
        out_specs.append(row_spec(LANE))
    body = functools.partial(_qkv_body, n_heads=n_heads, n_kv=n_kv, hd=hd,
                             q_scale=hd ** -0.5 * LOG2E, with_gate=with_gate, q_f32=q_f32,
                             attn_copies=attn_copies)
    return pl.pallas_call(body, grid=(rows // tm,), in_specs=in_specs, out_specs=out_specs,
                          out_shape=out_shape, compiler_params=_cparams(("arbitrary",)),
                          name="qkv_proj")(*ins)


def _fox_aug_body(lf_ref, qx_ref, kx_ref, carry_ref, *, n_heads, n_kv):
    t = pl.program_id(1)

    @pl.when(t == 0)
    def _():
        carry_ref[...] = jnp.zeros_like(carry_ref)

    lf = lf_ref[...]
    tc = lf.shape[0]
    tri = (lax.broadcasted_iota(jnp.int32, (tc, tc), 1)
           <= lax.broadcasted_iota(jnp.int32, (tc, tc), 0)).astype(F32)
    c = jnp.dot(tri, lf, preferred_element_type=F32, precision=lax.Precision.HIGHEST) + carry_ref[...]
    carry_ref[...] = c[tc - 1:tc, :]
    c2 = c * LOG2E
    lane = lax.broadcasted_iota(jnp.int32, (tc, LANE), 1)
    group = n_heads // n_kv
    ones3 = jnp.where(lane < 3, 1.0, 0.0)
    for kvh in range(n_kv):
        kaug = ones3
        for g in range(group):
            h = kvh * group + g
            hi, mid, lo = _split3(jnp.broadcast_to(c2[:, h:h + 1], (tc, LANE)))
            base = 3 + 3 * g
            own = jnp.where((lane >= base) & (lane < base + 3), 1.0, 0.0)
            qaug = jnp.where(lane == 0, hi, jnp.where(lane == 1, mid, jnp.where(lane == 2, lo, own)))
            qx_ref[:, h * LANE:(h + 1) * LANE] = qaug.astype(BF16)
            kaug = jnp.where(lane == base, -hi,
                             jnp.where(lane == base + 1, -mid, jnp.where(lane == base + 2, -lo, kaug)))
        kx_ref[:, kvh * LANE:(kvh + 1) * LANE] = kaug.astype(BF16)


def _fox_aug(lf128, *, batch, seq, n_heads, n_kv):
    tc = _row_tile(seq, 512)
    nt = seq // tc
    body = functools.partial(_fox_aug_body, n_heads=n_heads, n_kv=n_kv)
    return pl.pallas_call(
        body, grid=(batch, nt),
        in_specs=[pl.BlockSpec((tc, LANE), lambda b, t: (b * nt + t, 0))],
        out_specs=[pl.BlockSpec((tc, n_heads * LANE), lambda b, t: (b * nt + t, 0)),
                   pl.BlockSpec((tc, n_kv * LANE), lambda b, t: (b * nt + t, 0))],
        out_shape=[jax.ShapeDtypeStruct((batch * seq, n_heads * LANE), BF16),
                   jax.ShapeDtypeStruct((batch * seq, n_kv * LANE), BF16)],
        scratch_shapes=[pltpu.VMEM((1, LANE), F32)],
        compiler_params=_cparams(("arbitrary", "arbitrary")), name="fox_aug")(lf128)


def _kmean_body(k_ref, o_ref, *, nblk):
    k = k_ref[...]
    for j in range(nblk):
        o_ref[j:j + 1, :] = jnp.sum(k[j * MOBA_BLOCK:(j + 1) * MOBA_BLOCK, :], axis=0,
                                    keepdims=True) * (1.0 / MOBA_BLOCK)
    if o_ref.shape[0] > nblk:
        o_ref[nblk:, :] = jnp.zeros((o_ref.shape[0] - nblk, o_ref.shape[1]), F32)


def _kmean(k32, *, batch, seq):
    kd = k32.shape[1]
    nblk = seq // MOBA_BLOCK
    assert nblk <= COL_QPOS
    return pl.pallas_call(
        functools.partial(_kmean_body, nblk=nblk), grid=(batch,),
        in_specs=[pl.BlockSpec((seq, kd), lambda b: (b, 0))],
        out_specs=pl.BlockSpec((None, LANE, kd), lambda b: (b, 0, 0)),
        out_shape=jax.ShapeDtypeStruct((batch, LANE, kd), F32),
        compiler_params=_cparams(("arbitrary",)), name="moba_kmean")(k32)


def _top3_allowed(gate, lane, own):
    lane_f = lane.astype(F32)
    gm = jnp.where(lane < own, gate, -jnp.inf)
    sel = jnp.zeros(gate.shape, F32)
    for _ in range(MOBA_TOPK):
        mx = jnp.max(gm, axis=1, keepdims=True)
        idx = jnp.min(jnp.where(gm == mx, lane_f, float(LANE)), axis=1, keepdims=True)
        pick = lane_f == idx
        sel = jnp.where(pick, 1.0, sel)
        gm = jnp.where(pick, -jnp.inf, gm)
    return jnp.where(lane < own, sel, jnp.where(lane == own, 1.0, 0.0))


def _flash_body(qi_ref, kj_ref, last_ref, *refs, group, hd, moba, tile):
    if moba:
        q_ref, qx_ref, k_ref, kx_ref, vt_ref, kmt_ref, o_ref, qaug, m_sc, l_sc, acc_sc = refs
    else:
        q_ref, qx_ref, k_ref, kx_ref, vt_ref, o_ref, qaug, m_sc, l_sc, acc_sc = refs
    p = pl.program_id(2)
    i, j = qi_ref[p], kj_ref[p]

    @pl.when(j == i)
    def _():
        for g in range(group):
            rows = slice(g * tile, (g + 1) * tile)
            if moba:
                qf = q_ref[:, g * hd:(g + 1) * hd]
                gate = jnp.dot(qf, kmt_ref[...], preferred_element_type=F32,
                               precision=lax.Precision.HIGHEST)
                lane = lax.broadcasted_iota(jnp.int32, (tile, LANE), 1)
                pos = i * tile + lax.broadcasted_iota(jnp.int32, (tile, LANE), 0)
                own = lax.shift_right_logical(pos, int(math.log2(MOBA_BLOCK)))
                allowed = _top3_allowed(gate, lane, own)
                bias = jnp.where(allowed > 0.0, 0.0, NEG_BIG)
                aug = jnp.where(lane < COL_QPOS, bias, qx_ref[:, g * LANE:(g + 1) * LANE].astype(F32))
                qaug[rows, :hd] = (qf * (hd ** -0.5 * LOG2E)).astype(BF16)
                qaug[rows, hd:] = aug.astype(BF16)
            else:
                qaug[rows, :hd] = q_ref[:, g * hd:(g + 1) * hd]
                qaug[rows, hd:] = qx_ref[:, g * LANE:(g + 1) * LANE]
        m_sc[...] = jnp.full(m_sc.shape, -jnp.inf, F32)
        l_sc[...] = jnp.zeros(l_sc.shape, F32)
        acc_sc[...] = jnp.zeros(acc_sc.shape, F32)

    def step(diagonal):
        kaug = jnp.concatenate([k_ref[...], kx_ref[...]], axis=1)
        vt = vt_ref[...]
        n_chunks = group * tile // FLASH_CHUNK
        scores = [_dot_nt(kaug, qaug[0:FLASH_CHUNK, :])]
        for c in range(n_chunks):
            cols = slice(c * FLASH_CHUNK, (c + 1) * FLASH_CHUNK)
            if c + 1 < n_chunks:
                scores.append(_dot_nt(kaug, qaug[(c + 1) * FLASH_CHUNK:(c + 2) * FLASH_CHUNK, :]))
            st = scores[c]
            if diagonal:
                key = lax.broadcasted_iota(jnp.int32, st.shape, 0)
                query = (c * FLASH_CHUNK) % tile + lax.broadcasted_iota(jnp.int32, st.shape, 1)
                st = jnp.where(key <= query, st, -jnp.inf)
            m_prev = m_sc[:, cols]
            m_new = jnp.maximum(m_prev, jnp.max(st, axis=0, keepdims=True))
            alpha = jnp.exp2(m_prev - m_new)
            pt = jnp.exp2(st - m_new)
            l_sc[:, cols] = alpha * l_sc[:, cols] + jnp.sum(pt, axis=0, keepdims=True)
            acc_sc[:, cols] = alpha * acc_sc[:, cols] + jnp.dot(vt, pt.astype(BF16),
                                                                preferred_element_type=F32)
            m_sc[:, cols] = m_new

    pl.when(j == i)(lambda: step(True))
    pl.when(j != i)(lambda: step(False))

    @pl.when(last_ref[p] == 1)
    def _():
        out_t = acc_sc[...] / l_sc[...]
        for g in range(group):
            o_ref[:, g * hd:(g + 1) * hd] = out_t[:, g * tile:(g + 1) * tile].T.astype(o_ref.dtype)


def _flash_prompt(q, qx, kb, kx, vt, kmt, *, batch, seq, n_heads, n_kv, hd, moba):
    tile = ATTN_TILE
    assert seq % tile == 0 and tile % MOBA_BLOCK == 0 and tile & (tile - 1) == 0
    nt = seq // tile
    group = n_heads // n_kv
    qi = np.concatenate([np.full(i + 1, i) for i in range(nt)]).astype(np.int32)
    kj = np.concatenate([np.array([i] + list(range(i))) for i in range(nt)]).astype(np.int32)
    last = np.concatenate([np.array([0] * i + [1]) for i in range(nt)]).astype(np.int32)
    gw = group * hd
    q_map = lambda b, h, p, qi, kj, la: (b * nt + qi[p], h)
    k_map = lambda b, h, p, qi, kj, la: (b * nt + kj[p], h)
    vt_spec = pl.BlockSpec((hd, tile), lambda b, h, p, qi, kj, la: (h, b * nt + kj[p]))
    in_specs = [pl.BlockSpec((tile, gw), q_map)]
    if moba:
        in_specs += [pl.BlockSpec((tile, group * LANE), lambda b, h, p, qi, kj, la: (qi[p], h)),
                     pl.BlockSpec((tile, hd), k_map),
                     pl.BlockSpec((tile, LANE), lambda b, h, p, qi, kj, la: (kj[p], 0)),
                     vt_spec,
                     pl.BlockSpec((None, None, hd, LANE), lambda b, h, p, qi, kj, la: (b, h, 0, 0))]
        ins = [q, qx, kb, kx, vt, kmt]
    else:
        in_specs += [pl.BlockSpec((tile, group * LANE), q_map),
                     pl.BlockSpec((tile, hd), k_map),
                     pl.BlockSpec((tile, LANE), k_map),
                     vt_spec]
        ins = [q, qx, kb, kx, vt]
    body = functools.partial(_flash_body, group=group, hd=hd, moba=moba, tile=tile)
    grid_spec = pltpu.PrefetchScalarGridSpec(
        num_scalar_prefetch=3, grid=(batch, n_kv, len(qi)), in_specs=in_specs,
        out_specs=pl.BlockSpec((tile, gw), q_map),
        scratch_shapes=[pltpu.VMEM((group * tile, hd + LANE), BF16),
                        pltpu.VMEM((1, group * tile), F32), pltpu.VMEM((1, group * tile), F32),
                        pltpu.VMEM((hd, group * tile), F32)])
    return pl.pallas_call(
        body, grid_spec=grid_spec, out_shape=jax.ShapeDtypeStruct((batch * seq, n_heads * hd), BF16),
        compiler_params=_cparams(("arbitrary", "arbitrary", "arbitrary")),
        name="moba_prompt" if moba else "fox_prompt")(jnp.asarray(qi), jnp.asarray(kj),
                                                      jnp.asarray(last), *ins)


def _moba_static_cols(seq, n_heads):
    pos = jnp.arange(seq, dtype=jnp.int32)
    slopes = jnp.exp2(-8.0 * jnp.arange(1, n_heads + 1, dtype=F32) / n_heads) * LOG2E
    qparts = _split3(-(slopes[None, :] * pos.astype(F32)[:, None]))
    sparts = _split3(slopes)
    qx = jnp.zeros((seq, n_heads, LANE), F32)
    for c in range(3):
        qx = qx.at[:, :, COL_QPOS + c].set(qparts[c])
        qx = qx.at[:, :, COL_KPOS + c].set(jnp.broadcast_to(sparts[c][None, :], (seq, n_heads)))
        qx = qx.at[:, :, COL_KPOS + 3 + c].set(jnp.broadcast_to(sparts[c][None, :], (seq, n_heads)))
    blk = pos // MOBA_BLOCK
    lane = jnp.arange(LANE, dtype=jnp.int32)[None, :]
    kx = jnp.where(lane == blk[:, None], 1.0, 0.0)
    kx = jnp.where((lane >= COL_QPOS) & (lane < COL_QPOS + 3), 1.0, kx)
    kx = jnp.where((lane >= COL_KPOS) & (lane < COL_KPOS + 3), (blk * MOBA_BLOCK).astype(F32)[:, None], kx)
    kx = jnp.where((lane >= COL_KPOS + 3) & (lane < COL_KPOS + 6), (pos % MOBA_BLOCK).astype(F32)[:, None], kx)
    return qx.reshape(seq, n_heads * LANE).astype(BF16), kx.astype(BF16)


def _page_specs(n, block, layer_arg, n_pages, pages_per_step):
    def spec(i):
        def imap(b, s, pt):
            return (layer_arg, pt[b * n_pages + s * pages_per_step + i]) + (0,) * (len(block) - 2)
        return pl.BlockSpec(block, imap)
    return [spec(i) for i in range(n)]


def _cum_sample_body(pt_ref, *refs, pg):
    lf_refs, new_ref, c_ref, cnew_ref, carry_ref = refs[:pg], refs[pg], refs[pg + 1], refs[pg + 2], refs[pg + 3]
    s = pl.program_id(1)

    @pl.when(s == 0)
    def _():
        carry_ref[...] = jnp.zeros_like(carry_ref)

    triu = (lax.broadcasted_iota(jnp.int32, (LANE, LANE), 0)
            <= lax.broadcasted_iota(jnp.int32, (LANE, LANE), 1)).astype(F32)
    carry = carry_ref[...]
    nh = carry.shape[0]
    local = jnp.dot(jnp.concatenate([r[...] for r in lf_refs], axis=0), triu,
                    preferred_element_type=F32, precision=lax.Precision.HIGHEST)
    for i in range(pg):
        loc = local[i * nh:(i + 1) * nh, :]
        c_ref[:, i * LANE:(i + 1) * LANE] = loc + carry
        carry = carry + jnp.broadcast_to(loc[:, LANE - 1:LANE], loc.shape)
    carry_ref[...] = carry

    @pl.when(s == pl.num_programs(1) - 1)
    def _():
        cnew_ref[...] = jnp.dot(new_ref[...], triu, preferred_element_type=F32,
                                precision=lax.Precision.HIGHEST) + carry


def _cum_sample(page_table, logf_t, lfnew_t, *, n_heads):
    db, n_pages = page_table.shape
    page = logf_t.shape[2]
    assert page == LANE
    pg = math.gcd(PAGES_PER_STEP, n_pages)
    in_specs = _page_specs(pg, (None, None, n_heads, page), 0, n_pages, pg)
    in_specs.append(pl.BlockSpec((None, n_heads, LANE), lambda b, s, pt: (b, 0, 0)))
    grid_spec = pltpu.PrefetchScalarGridSpec(
        num_scalar_prefetch=1, grid=(db, n_pages // pg), in_specs=in_specs,
        out_specs=[pl.BlockSpec((None, n_heads, pg * page), lambda b, s, pt: (b, 0, s)),
                   pl.BlockSpec((None, n_heads, LANE), lambda b, s, pt: (b, 0, 0))],
        scratch_shapes=[pltpu.VMEM((n_heads, LANE), F32)])
    return pl.pallas_call(
        functools.partial(_cum_sample_body, pg=pg), grid_spec=grid_spec,
        out_shape=[jax.ShapeDtypeStruct((db, n_heads, n_pages * page), F32),
                   jax.ShapeDtypeStruct((db, n_heads, LANE), F32)],
        compiler_params=_cparams(("arbitrary", "arbitrary")),
        name="fox_sample_cumsum")(page_table.reshape(-1), *([logf_t[None]] * pg), lfnew_t)


def _row_token(shape, n_heads, dec_seq):
    row = lax.broadcasted_iota(jnp.int32, shape, 0)
    tok = jnp.zeros(shape, jnp.int32)
    for t in range(1, dec_seq):
        tok = tok + (row >= t * n_heads).astype(jnp.int32)
    return tok


def _stage_pages(k_refs, v_refs, kb, vb, page, n_kv, want_ksum=False):
    hd = kb.shape[1] // n_kv
    ksums = []
    for i, (k_ref, v_ref) in enumerate(zip(k_refs, v_refs)):
        parts = []
        for h in range(n_kv):
            kf = k_ref[pl.ds(h, page, stride=n_kv), :]
            kb[i * page:(i + 1) * page, h * hd:(h + 1) * hd] = kf.astype(BF16)
            vb[i * page:(i + 1) * page, h * hd:(h + 1) * hd] = (
                v_ref[pl.ds(h, page, stride=n_kv), :].astype(BF16))
            if want_ksum:
                parts.append(jnp.sum(kf, axis=0, keepdims=True))
        if want_ksum:
            ksums.append(jnp.concatenate(parts, axis=1))
    return ksums


def _fox_sample_body(pt_ref, *refs, pg, page, dec_seq, n_heads, n_kv):
    k_refs, v_refs = refs[:pg], refs[pg:2 * pg]
    (q_ref, cq_ref, c_ref, cnew_ref, kn_ref, vn_ref, o_ref, kb, vb, m_sc, l_sc, acc_sc) = refs[2 * pg:]
    s = pl.program_id(1)

    @pl.when(s == 0)
    def _():
        m_sc[...] = jnp.full(m_sc.shape, -jnp.inf, F32)
        l_sc[...] = jnp.zeros(l_sc.shape, F32)
        acc_sc[...] = jnp.zeros(acc_sc.shape, F32)

    def update(logits, v):
        m_prev = m_sc[...]
        m_new = jnp.maximum(m_prev, jnp.max(logits, axis=1, keepdims=True))
        alpha = jnp.exp2(m_prev - m_new)
        pr = jnp.exp2(logits - m_new)
        l_sc[...] = alpha * l_sc[...] + jnp.sum(pr, axis=1, keepdims=True)
        acc_sc[...] = alpha * acc_sc[...] + jnp.dot(pr.astype(BF16), v, preferred_element_type=F32)
        m_sc[...] = m_new

    q = q_ref[...]
    cq = cq_ref[...]
    _stage_pages(k_refs, v_refs, kb, vb, page, n_kv)
    ck = jnp.concatenate([c_ref[...]] * dec_seq, axis=0)
    update(_dot_nt(q, kb[...]) + (cq - ck) * LOG2E, vb[...])

    @pl.when(s == pl.num_programs(1) - 1)
    def _():
        rows = q.shape[0]
        ckn = jnp.concatenate([cnew_ref[...]] * dec_seq, axis=0)
        logits = _dot_nt(q, kn_ref[...].astype(BF16)) + (cq - ckn) * LOG2E
        tok = _row_token((rows, LANE), n_heads, dec_seq)
        logits = jnp.where(lax.broadcasted_iota(jnp.int32, (rows, LANE), 1) <= tok, logits, -jnp.inf)
        update(logits, vn_ref[...].astype(BF16))
        o_ref[...] = acc_sc[...] / l_sc[...]


def _moba_sample_body(pt_ref, *refs, pg, page, dec_seq, n_heads, n_kv, past):
    k_refs, v_refs = refs[:pg], refs[pg:2 * pg]
    (q_ref, q32_ref, slope_ref, kn_ref, vn_ref, o_ref,
     kb, vb, acc_all, m_all, l_all, km) = refs[2 * pg:]
    s = pl.program_id(1)
    nblk = past // MOBA_BLOCK
    ppb = MOBA_BLOCK // page
    rows = q_ref.shape[0]
    lane = lax.broadcasted_iota(jnp.int32, (rows, LANE), 1)
    qpos = (past + _row_token((rows, 1), n_heads, dec_seq)).astype(F32)
    km_row = lax.broadcasted_iota(jnp.int32, km.shape, 0)

    @pl.when(s == 0)
    def _():
        m_all[...] = jnp.full(m_all.shape, -jnp.inf, F32)
        l_all[...] = jnp.zeros(l_all.shape, F32)
        km[...] = jnp.zeros(km.shape, F32)

    def block_partial(logits, v, blk):
        mj = jnp.max(logits, axis=1, keepdims=True)
        pr = jnp.exp2(logits - mj)
        acc_all[blk] = jnp.dot(pr.astype(BF16), v, preferred_element_type=F32)
        m_all[...] = jnp.where(lane == blk, mj, m_all[...])
        l_all[...] = jnp.where(lane == blk, jnp.sum(pr, axis=1, keepdims=True), l_all[...])

    q = q_ref[...]
    slope = slope_ref[...]
    ksums = _stage_pages(k_refs, v_refs, kb, vb, page, n_kv, want_ksum=True)
    width = pg * page
    kpos = (s * width + lax.broadcasted_iota(jnp.int32, (1, width), 1)).astype(F32)
    logits = _dot_nt(q, kb[...]) - slope * (qpos - kpos)
    for jb in range(pg // ppb):
        blk = s * (pg // ppb) + jb
        block_partial(logits[:, jb * MOBA_BLOCK:(jb + 1) * MOBA_BLOCK],
                      vb[jb * MOBA_BLOCK:(jb + 1) * MOBA_BLOCK, :], blk)
        ksum = ksums[jb * ppb]
        for i in range(1, ppb):
            ksum = ksum + ksums[jb * ppb + i]
        km[...] = jnp.where(km_row == blk, ksum * (1.0 / MOBA_BLOCK), km[...])

    @pl.when(s == pl.num_programs(1) - 1)
    def _():
        npos = (past + lax.broadcasted_iota(jnp.int32, (1, LANE), 1)).astype(F32)
        ln = _dot_nt(q, kn_ref[...].astype(BF16)) - slope * (qpos - npos)
        ln = jnp.where(npos <= qpos, ln, -jnp.inf)
        block_partial(ln, vn_ref[...].astype(BF16), nblk)
        gate = lax.dot_general(q32_ref[...], km[...], (((1,), (1,)), ((), ())),
                               preferred_element_type=F32, precision=lax.Precision.HIGHEST)
        allowed = _top3_allowed(gate, lane, jnp.full((rows, LANE), nblk, jnp.int32)) > 0.0
        m_row = jnp.max(jnp.where(allowed, m_all[...], -jnp.inf), axis=1, keepdims=True)
        w = jnp.where(allowed, jnp.exp2(m_all[...] - m_row), 0.0)
        denom = jnp.sum(w * l_all[...], axis=1, keepdims=True)
        out = jnp.zeros(o_ref.shape, F32)
        for jb in range(nblk + 1):
            out = out + w[:, jb:jb + 1] * acc_all[jb]
        o_ref[...] = out / denom


def _sample_attention(page_table, cache_k4, cache_v4, layer, qbd, kn_pad, vn_pad, extra, *, moba,
                      dec_seq, n_heads):
    db, n_pages = page_table.shape
    kd = qbd.shape[2]
    hd = cache_k4.shape[3]
    n_kv = kd // hd
    page = cache_k4.shape[2] // n_kv
    rows = qbd.shape[1]
    past = n_pages * page
    pg = math.gcd(PAGES_PER_STEP, n_pages)
    assert page == LANE and MOBA_BLOCK % page == 0 and pg % (MOBA_BLOCK // page) == 0
    assert past % MOBA_BLOCK == 0 and past // MOBA_BLOCK < LANE
    page_specs = (_page_specs(pg, (None, None, page * n_kv, hd), layer, n_pages, pg)
                  + _page_specs(pg, (None, None, page * n_kv, hd), layer, n_pages, pg))
    per_b = lambda a: pl.BlockSpec((None,) + a.shape[1:], lambda b, s, pt: (b,) + (0,) * (a.ndim - 1))
    common = [pltpu.VMEM((pg * page, kd), BF16), pltpu.VMEM((pg * page, kd), BF16)]
    if moba:
        q32, slope = extra
        ins = [qbd, q32, slope, kn_pad, vn_pad]
        in_specs = page_specs + [per_b(qbd), per_b(q32),
                                 pl.BlockSpec(slope.shape, lambda b, s, pt: (0, 0)),
                                 per_b(kn_pad), per_b(vn_pad)]
        scratch = common + [pltpu.VMEM((past // MOBA_BLOCK + 1, rows, kd), F32),
                            pltpu.VMEM((rows, LANE), F32), pltpu.VMEM((rows, LANE), F32),
                            pltpu.VMEM((LANE, kd), F32)]
        body = functools.partial(_moba_sample_body, pg=pg, page=page, dec_seq=dec_seq,
                                 n_heads=n_heads, n_kv=n_kv, past=past)
    else:
        cq, c_t, cnew_t = extra
        ins = [qbd, cq, c_t, cnew_t, kn_pad, vn_pad]
        in_specs = page_specs + [per_b(qbd), per_b(cq),
                                 pl.BlockSpec((None, n_heads, pg * page), lambda b, s, pt: (b, 0, s)),
                                 per_b(cnew_t), per_b(kn_pad), per_b(vn_pad)]
        scratch = common + [pltpu.VMEM((rows, 1), F32), pltpu.VMEM((rows, 1), F32),
                            pltpu.VMEM((rows, kd), F32)]
        body = functools.partial(_fox_sample_body, pg=pg, page=page, dec_seq=dec_seq, n_heads=n_heads,
                                 n_kv=n_kv)
    grid_spec = pltpu.PrefetchScalarGridSpec(
        num_scalar_prefetch=1, grid=(db, n_pages // pg), in_specs=in_specs,
        out_specs=pl.BlockSpec((None, rows, kd), lambda b, s, pt: (b, 0, 0)), scratch_shapes=scratch)
    return pl.pallas_call(
        body, grid_spec=grid_spec, out_shape=jax.ShapeDtypeStruct((db, rows, kd), F32),
        compiler_params=_cparams(("arbitrary", "arbitrary")),
        name="moba_sample" if moba else "fox_sample")(
            page_table.reshape(-1), *([cache_k4] * pg), *([cache_v4] * pg), *ins)


def _wo_body(x_ref, mix_ref, w_ref, o_ref):
    o_ref[...] = x_ref[...] + jnp.dot(mix_ref[...], w_ref[...], preferred_element_type=F32)


def _wo_residual(x, mix_bf, wo_bf, *, tm):
    rows, d = x.shape
    tm = _row_tile(rows, tm)
    return pl.pallas_call(
        _wo_body, grid=(rows // tm,),
        in_specs=[pl.BlockSpec((tm, d), lambda i: (i, 0)),
                  pl.BlockSpec((tm, mix_bf.shape[1]), lambda i: (i, 0)),
                  pl.BlockSpec(wo_bf.shape, lambda i: (0, 0))],
        out_specs=pl.BlockSpec((tm, d), lambda i: (i, 0)),
        out_shape=jax.ShapeDtypeStruct((rows, d), F32),
        compiler_params=_cparams(("arbitrary",)), name="wo_residual")(x, mix_bf, wo_bf)


def _mlp_body(h_ref, g_ref, wu_ref, wd_ref, o_ref, hn_sc):
    f = pl.program_id(1)

    @pl.when(f == 0)
    def _():
        h = h_ref[...]
        ms = jnp.mean(h * h, axis=-1, keepdims=True)
        hn_sc[...] = (h * lax.rsqrt(ms + RMS_EPS) * g_ref[...]).astype(BF16)
        o_ref[...] = h

    u = jnp.maximum(jnp.dot(hn_sc[...], wu_ref[...], preferred_element_type=F32), 0.0)
    o_ref[...] += jnp.dot((u * u).astype(BF16), wd_ref[...], preferred_element_type=F32)


def _mlp_residual(h, g_mlp, wu_bf, wd_bf, *, tm, tf):
    rows, d = h.shape
    dff = wu_bf.shape[1]
    tm = _row_tile(rows, tm)
    tf = min(tf, dff)
    assert dff % tf == 0
    return pl.pallas_call(
        _mlp_body, grid=(rows // tm, dff // tf),
        in_specs=[pl.BlockSpec((tm, d), lambda i, f: (i, 0)),
                  pl.BlockSpec(g_mlp.shape, lambda i, f: (0, 0)),
                  pl.BlockSpec((d, tf), lambda i, f: (0, f)),
                  pl.BlockSpec((tf, d), lambda i, f: (f, 0))],
        out_specs=pl.BlockSpec((tm, d), lambda i, f: (i, 0)),
        out_shape=jax.ShapeDtypeStruct((rows, d), F32),
        scratch_shapes=[pltpu.VMEM((tm, d), BF16)],
        compiler_params=_cparams(("arbitrary", "arbitrary")), name="mlp_residual")(h, g_mlp, wu_bf, wd_bf)


def kernel(x_prompt, x_sample, cache_k, cache_v, cache_logf, page_table, attn_norm, w_qkv, q_norm, k_norm,
           w_fgate, b_fgate, w_o, mlp_norm, w_up, w_down):
    batch, seq, d = x_prompt.shape
    db, dec_seq, _ = x_sample.shape
    depth, n_pool, page, n_kv, hd = cache_k.shape
    n_heads = w_fgate.shape[2]
    group = n_heads // n_kv
    qd, kd = n_heads * hd, n_kv * hd
    n_pages = page_table.shape[1]
    past = n_pages * page
    rows_s = dec_seq * n_heads

    hp = x_prompt.reshape(batch * seq, d)
    hs = x_sample.reshape(db * dec_seq, d)
    cache_k4 = cache_k.reshape(depth, n_pool, page * n_kv, hd)
    cache_v4 = cache_v.reshape(depth, n_pool, page * n_kv, hd)
    row_kv = (jnp.arange(rows_s) % n_heads) // group
    row_mask = (row_kv[:, None] == jnp.arange(n_kv)[None, :]).astype(F32)
    slopes = jnp.exp2(-8.0 * jnp.arange(1, n_heads + 1, dtype=F32) / n_heads)

    def block_diag(q2d, dtype):
        q = q2d.astype(F32).reshape(db, rows_s, 1, hd) * row_mask[None, :, :, None]
        return q.reshape(db, rows_s, kd).astype(dtype)

    def pad_page(a2d):
        a = a2d.reshape(db, dec_seq, kd)
        return jnp.pad(a, ((0, 0), (0, LANE - dec_seq), (0, 0)))

    def own_lanes(o):
        o6 = o.reshape(db, dec_seq, n_kv, group, n_kv, hd)
        return jnp.stack([o6[:, :, k, :, k, :] for k in range(n_kv)], axis=2).reshape(db * dec_seq, qd)

    outs = {k: [] for k in ("kp", "vp", "fp", "ks", "vs", "fs")}
    for i in range(depth):
        fox = i % 2 == 0
        j = i // 2
        w_bf = w_qkv[i].astype(BF16)
        g_attn, qg, kg = attn_norm[i][None, :], q_norm[i][None, :], k_norm[i][None, :]
        if fox:
            wf_bf = jnp.pad(w_fgate[j], ((0, 0), (0, LANE - n_heads))).astype(BF16)
            bf_pad = jnp.pad(b_fgate[j], (0, LANE - n_heads))[None, :]
        else:
            wf_bf = bf_pad = None
        proj = functools.partial(_qkv_proj, g_attn=g_attn, w_bf=w_bf, q_gain=qg, k_gain=kg, wf_bf=wf_bf,
                                 bf_pad=bf_pad, n_heads=n_heads, n_kv=n_kv, hd=hd, q_f32=not fox)
        res_p = proj(hp, attn_copies=True, tm=512)
        res_s = proj(hs, attn_copies=False, tm=128)
        qp, kp32, vp32, kpb, vpt = res_p[:5]
        qs, ks32, vs32 = res_s[:3]
        outs["kp"].append(kp32.reshape(batch, seq, n_kv, hd))
        outs["vp"].append(vp32.reshape(batch, seq, n_kv, hd))
        outs["ks"].append(ks32.reshape(db, dec_seq, n_kv, hd))
        outs["vs"].append(vs32.reshape(db, dec_seq, n_kv, hd))
        kn_pad, vn_pad = pad_page(ks32), pad_page(vs32)
        if fox:
            lfp, lfs = res_p[5], res_s[3]
            outs["fp"].append(lfp[:, :n_heads].reshape(batch, seq, n_heads))
            outs["fs"].append(lfs[:, :n_heads].reshape(db, dec_seq, n_heads))
            qx, kx = _fox_aug(lfp, batch=batch, seq=seq, n_heads=n_heads, n_kv=n_kv)
            mix_p = _flash_prompt(qp, qx, kpb, kx, vpt, None, batch=batch, seq=seq, n_heads=n_heads,
                                  n_kv=n_kv, hd=hd, moba=False)
            logf_t = jnp.swapaxes(cache_logf[j], 1, 2)
            lfnew_t = jnp.pad(jnp.swapaxes(lfs[:, :n_heads].reshape(db, dec_seq, n_heads), 1, 2),
                              ((0, 0), (0, 0), (0, LANE - dec_seq)))
            c_t, cnew_t = _cum_sample(page_table, logf_t, lfnew_t, n_heads=n_heads)
            cq = jnp.swapaxes(cnew_t[:, :, :dec_seq], 1, 2).reshape(db, rows_s, 1)
            o_s = _sample_attention(page_table, cache_k4, cache_v4, i, block_diag(qs, BF16), kn_pad, vn_pad,
                                    (cq, c_t, cnew_t), moba=False, dec_seq=dec_seq, n_heads=n_heads)
        else:
            kmean = _kmean(kp32, batch=batch, seq=seq)
            kmt = jnp.swapaxes(kmean.reshape(batch, LANE, n_kv, hd), 1, 3)
            kmt = jnp.swapaxes(kmt, 1, 2)
            qx, kx = _moba_static_cols(seq, n_heads)
            mix_p = _flash_prompt(qp, qx, kpb, kx, vpt, kmt, batch=batch, seq=seq, n_heads=n_heads,
                                  n_kv=n_kv, hd=hd, moba=True)
            slope_col = (jnp.tile(slopes, dec_seq) * LOG2E)[:, None]
            q_scaled = qs * (hd ** -0.5 * LOG2E)
            o_s = _sample_attention(page_table, cache_k4, cache_v4, i, block_diag(q_scaled, BF16), kn_pad,
                                    vn_pad, (block_diag(qs, F32), slope_col), moba=True, dec_seq=dec_seq,
                                    n_heads=n_heads)
        mix_s = own_lanes(o_s).astype(BF16)
        wo_bf = w_o[i].astype(BF16)
        hp = _wo_residual(hp, mix_p, wo_bf, tm=512)
        hs = _wo_residual(hs, mix_s, wo_bf, tm=128)
        wu_bf, wd_bf = w_up[i].astype(BF16), w_down[i].astype(BF16)
        g_mlp = mlp_norm[i][None, :]
        hp = _mlp_residual(hp, g_mlp, wu_bf, wd_bf, tm=512, tf=512)
        hs = _mlp_residual(hs, g_mlp, wu_bf, wd_bf, tm=128, tf=512)
    return (hp.reshape(batch, seq, d), hs.reshape(db, dec_seq, d),
            jnp.stack(outs["kp"]), jnp.stack(outs["vp"]), jnp.stack(outs["fp"]),
            jnp.stack(outs["ks"]), jnp.stack(outs["vs"]), jnp.stack(outs["fs"]))
```

```python
import functools
import math

import numpy as np
import jax
import jax.numpy as jnp
from jax import lax
from jax.experimental import pallas as pl
from jax.experimental.pallas import tpu as pltpu

F32 = jnp.float32
BF16 = jnp.bfloat16
LOG2E = 1.4426950408889634
RMS_EPS = 1e-6
MOBA_BLOCK = 256
MOBA_TOPK = 3
LANE = 128
ATTN_TILE = 512
FLASH_CHUNK = 512
PAGES_PER_STEP = 32
NEG_BIG = -(2.0 ** 100)
VMEM_LIMIT = 56 * 1024 * 1024
COL_BLK = 0
COL_QPOS = 64
COL_KPOS = 67


def _cparams(sem):
    return pltpu.CompilerParams(dimension_semantics=sem, vmem_limit_bytes=VMEM_LIMIT)


def _row_tile(rows, pref):
    t = min(rows, pref)
    while rows % t:
        t -= 8
    return t


def _split3(x):
    hi = x.astype(BF16).astype(F32)
    r1 = x - hi
    mid = r1.astype(BF16).astype(F32)
    lo = (r1 - mid).astype(BF16).astype(F32)
    return hi, mid, lo


def _dot_nt(a, b):
    return lax.dot_general(a, b, (((1,), (1,)), ((), ())), preferred_element_type=F32)


def _qkv_body(*refs, n_heads, n_kv, hd, q_scale, with_gate, q_f32, attn_copies):
    n_in = 7 if with_gate else 5
    x_ref, g_ref, w_ref, qg_ref, kg_ref = refs[:5]
    outs = list(refs[n_in:])
    q_ref, k32_ref, v32_ref = outs[:3]
    del outs[:3]
    if attn_copies:
        kb_ref, vt_ref = outs[:2]
        del outs[:2]
    qd, kd = n_heads * hd, n_kv * hd
    x = x_ref[...]
    ms = jnp.mean(x * x, axis=-1, keepdims=True)
    xn = (x * lax.rsqrt(ms + RMS_EPS) * g_ref[...]).astype(BF16)

    def head_norm(a, gain):
        m = jnp.mean(a * a, axis=-1, keepdims=True)
        return a * lax.rsqrt(m + RMS_EPS) * gain

    chunk = 4 * hd
    for c in range(qd // chunk):
        acc = jnp.dot(xn, w_ref[:, c * chunk:(c + 1) * chunk], preferred_element_type=F32)
        for j in range(4):
            qn = head_norm(acc[:, j * hd:(j + 1) * hd], qg_ref[...])
            lo = c * chunk + j * hd
            if q_f32:
                q_ref[:, lo:lo + hd] = qn
            else:
                q_ref[:, lo:lo + hd] = (qn * q_scale).astype(BF16)
    tm = x.shape[0]
    acc = jnp.dot(xn, w_ref[:, qd:qd + kd], preferred_element_type=F32)
    for j in range(n_kv):
        kn = head_norm(acc[:, j * hd:(j + 1) * hd], kg_ref[...])
        k32_ref[pl.ds(j, tm, stride=n_kv), :] = kn
        if attn_copies:
            kb_ref[:, j * hd:(j + 1) * hd] = kn.astype(BF16)
    acc = jnp.dot(xn, w_ref[:, qd + kd:qd + 2 * kd], preferred_element_type=F32)
    for j in range(n_kv):
        v32_ref[pl.ds(j, tm, stride=n_kv), :] = acc[:, j * hd:(j + 1) * hd]
    if attn_copies:
        vt_ref[...] = acc.T.astype(BF16)
    if with_gate:
        wf_ref, bf_ref = refs[5:7]
        z = jnp.dot(xn, wf_ref[...], preferred_element_type=F32) + bf_ref[...]
        lf = jnp.minimum(z, 0.0) - jnp.log1p(jnp.exp(-jnp.abs(z)))
        outs[0][...] = lf
        outs[1][...] = lf[:, :n_heads]


def _qkv_proj(x, g_attn, w_bf, q_gain, k_gain, wf_bf, bf_pad, *, n_heads, n_kv, hd, q_f32, attn_copies, tm):
    rows, d = x.shape
    qd, kd = n_heads * hd, n_kv * hd
    with_gate = wf_bf is not None
    tm = _row_tile(rows, tm)
    row_spec = lambda w: pl.BlockSpec((tm, w), lambda i: (i, 0))
    full = lambda a: pl.BlockSpec(a.shape, lambda i: (0,) * a.ndim)
    ins = [x, g_attn, w_bf, q_gain, k_gain]
    in_specs = [row_spec(d), full(g_attn), full(w_bf), full(q_gain), full(k_gain)]
    kv_spec = pl.BlockSpec((tm * n_kv, hd), lambda i: (i, 0))
    out_shape = [jax.ShapeDtypeStruct((rows, qd), F32 if q_f32 else BF16),
                 jax.ShapeDtypeStruct((rows * n_kv, hd), F32), jax.ShapeDtypeStruct((rows * n_kv, hd), F32)]
    out_specs = [row_spec(qd), kv_spec, kv_spec]
    if attn_copies:
        out_shape += [jax.ShapeDtypeStruct((rows, kd), BF16), jax.ShapeDtypeStruct((kd, rows), BF16)]
        out_specs += [row_spec(kd), pl.BlockSpec((kd, tm), lambda i: (0, i))]
    if with_gate:
        ins += [wf_bf, bf_pad]
        in_specs += [full(wf_bf), full(bf_pad)]
        out_shape += [jax.ShapeDtypeStruct((rows, LANE), F32), jax.ShapeDtypeStruct((rows, n_heads), F32)]
        out_specs += [row_spec(LANE), row_spec(n_heads)]
    body = functools.partial(_qkv_body, n_heads=n_heads, n_kv=n_kv, hd=hd,
                             q_scale=hd ** -0.5 * LOG2E, with_gate=with_gate, q_f32=q_f32,
                             attn_copies=attn_copies)
    return pl.pallas_call(body, grid=(rows // tm,), in_specs=in_specs, out_specs=out_specs,
                          out_shape=out_shape, compiler_params=_cparams(("arbitrary",)),
                          name="qkv_proj")(*ins)


def _fox_aug_body(lf_ref, qx_ref, kx_ref, carry_ref, *, n_heads, n_kv):
    t = pl.program_id(1)

    @pl.when(t == 0)
    def _():
        carry_ref[...] = jnp.zeros_like(carry_ref)

    lf = lf_ref[...]
    tc = lf.shape[0]
    tri = (lax.broadcasted_iota(jnp.int32, (tc, tc), 1)
           <= lax.broadcasted_iota(jnp.int32, (tc, tc), 0)).astype(F32)
    c = jnp.dot(tri, lf, preferred_element_type=F32, precision=lax.Precision.HIGHEST) + carry_ref[...]
    carry_ref[...] = c[tc - 1:tc, :]
    c2 = c * LOG2E
    lane = lax.broadcasted_iota(jnp.int32, (tc, LANE), 1)
    group = n_heads // n_kv
    ones3 = jnp.where(lane < 3, 1.0, 0.0)
    for kvh in range(n_kv):
        kaug = ones3
        for g in range(group):
            h = kvh * group + g
            hi, mid, lo = _split3(jnp.broadcast_to(c2[:, h:h + 1], (tc, LANE)))
            base = 3 + 3 * g
            own = jnp.where((lane >= base) & (lane < base + 3), 1.0, 0.0)
            qaug = jnp.where(lane == 0, hi, jnp.where(lane == 1, mid, jnp.where(lane == 2, lo, own)))
            qx_ref[:, h * LANE:(h + 1) * LANE] = qaug.astype(BF16)
            kaug = jnp.where(lane == base, -hi,
                             jnp.where(lane == base + 1, -mid, jnp.where(lane == base + 2, -lo, kaug)))
        kx_ref[:, kvh * LANE:(kvh + 1) * LANE] = kaug.astype(BF16)


def _fox_aug(lf128, *, batch, seq, n_heads, n_kv):
    tc = _row_tile(seq, 512)
    nt = seq // tc
    body = functools.partial(_fox_aug_body, n_heads=n_heads, n_kv=n_kv)
    return pl.pallas_call(
        body, grid=(batch, nt),
        in_specs=[pl.BlockSpec((tc, LANE), lambda b, t: (b * nt + t, 0))],
        out_specs=[pl.BlockSpec((tc, n_heads * LANE), lambda b, t: (b * nt + t, 0)),
                   pl.BlockSpec((tc, n_kv * LANE), lambda b, t: (b * nt + t, 0))],
        out_shape=[jax.ShapeDtypeStruct((batch * seq, n_heads * LANE), BF16),
                   jax.ShapeDtypeStruct((batch * seq, n_kv * LANE), BF16)],
        scratch_shapes=[pltpu.VMEM((1, LANE), F32)],
        compiler_params=_cparams(("arbitrary", "arbitrary")), name="fox_aug")(lf128)


def _kmean_body(k_ref, o_ref, *, nblk, n_kv):
    hd = k_ref.shape[1]
    for j in range(nblk):
        for h in range(n_kv):
            blk = k_ref[pl.ds(j * MOBA_BLOCK * n_kv + h, MOBA_BLOCK, stride=n_kv), :]
            o_ref[j:j + 1, h * hd:(h + 1) * hd] = jnp.sum(blk, axis=0, keepdims=True) * (1.0 / MOBA_BLOCK)
    if o_ref.shape[0] > nblk:
        o_ref[nblk:, :] = jnp.zeros((o_ref.shape[0] - nblk, o_ref.shape[1]), F32)


def _kmean(k32, *, batch, seq, n_kv):
    hd = k32.shape[1]
    nblk = seq // MOBA_BLOCK
    assert nblk <= COL_QPOS
    return pl.pallas_call(
        functools.partial(_kmean_body, nblk=nblk, n_kv=n_kv), grid=(batch,),
        in_specs=[pl.BlockSpec((seq * n_kv, hd), lambda b: (b, 0))],
        out_specs=pl.BlockSpec((None, LANE, n_kv * hd), lambda b: (b, 0, 0)),
        out_shape=jax.ShapeDtypeStruct((batch, LANE, n_kv * hd), F32),
        compiler_params=_cparams(("arbitrary",)), name="moba_kmean")(k32)


def _top3_allowed(gate, lane, own):
    lane_f = lane.astype(F32)
    gm = jnp.where(lane < own, gate, -jnp.inf)
    sel = jnp.zeros(gate.shape, F32)
    for _ in range(MOBA_TOPK):
        mx = jnp.max(gm, axis=1, keepdims=True)
        idx = jnp.min(jnp.where(gm == mx, lane_f, float(LANE)), axis=1, keepdims=True)
        pick = lane_f == idx
        sel = jnp.where(pick, 1.0, sel)
        gm = jnp.where(pick, -jnp.inf, gm)
    return jnp.where(lane < own, sel, jnp.where(lane == own, 1.0, 0.0))


def _flash_body(qi_ref, kj_ref, last_ref, *refs, group, hd, moba, tile):
    if moba:
        q_ref, qx_ref, k_ref, kx_ref, vt_ref, kmt_ref, o_ref, qaug, m_sc, l_sc, acc_sc = refs
    else:
        q_ref, qx_ref, k_ref, kx_ref, vt_ref, o_ref, qaug, m_sc, l_sc, acc_sc = refs
    p = pl.program_id(2)
    i, j = qi_ref[p], kj_ref[p]

    @pl.when(j == i)
    def _():
        for g in range(group):
            rows = slice(g * tile, (g + 1) * tile)
            if moba:
                qf = q_ref[:, g * hd:(g + 1) * hd]
                gate = jnp.dot(qf, kmt_ref[...], preferred_element_type=F32,
                               precision=lax.Precision.HIGHEST)
                lane = lax.broadcasted_iota(jnp.int32, (tile, LANE), 1)
                pos = i * tile + lax.broadcasted_iota(jnp.int32, (tile, LANE), 0)
                own = lax.shift_right_logical(pos, int(math.log2(MOBA_BLOCK)))
                allowed = _top3_allowed(gate, lane, own)
                bias = jnp.where(allowed > 0.0, 0.0, NEG_BIG)
                aug = jnp.where(lane < COL_QPOS, bias, qx_ref[:, g * LANE:(g + 1) * LANE].astype(F32))
                qaug[rows, :hd] = (qf * (hd ** -0.5 * LOG2E)).astype(BF16)
                qaug[rows, hd:] = aug.astype(BF16)
            else:
                qaug[rows, :hd] = q_ref[:, g * hd:(g + 1) * hd]
                qaug[rows, hd:] = qx_ref[:, g * LANE:(g + 1) * LANE]
        m_sc[...] = jnp.full(m_sc.shape, -jnp.inf, F32)
        l_sc[...] = jnp.zeros(l_sc.shape, F32)
        acc_sc[...] = jnp.zeros(acc_sc.shape, F32)

    def step(diagonal):
        kaug = jnp.concatenate([k_ref[...], kx_ref[...]], axis=1)
        vt = vt_ref[...]
        n_chunks = group * tile // FLASH_CHUNK
        scores = [_dot_nt(kaug, qaug[0:FLASH_CHUNK, :])]
        for c in range(n_chunks):
            cols = slice(c * FLASH_CHUNK, (c + 1) * FLASH_CHUNK)
            if c + 1 < n_chunks:
                scores.append(_dot_nt(kaug, qaug[(c + 1) * FLASH_CHUNK:(c + 2) * FLASH_CHUNK, :]))
            st = scores[c]
            if diagonal:
                key = lax.broadcasted_iota(jnp.int32, st.shape, 0)
                query = (c * FLASH_CHUNK) % tile + lax.broadcasted_iota(jnp.int32, st.shape, 1)
                st = jnp.where(key <= query, st, -jnp.inf)
            m_prev = m_sc[:, cols]
            m_new = jnp.maximum(m_prev, jnp.max(st, axis=0, keepdims=True))
            alpha = jnp.exp2(m_prev - m_new)
            pt = jnp.exp2(st - m_new)
            l_sc[:, cols] = alpha * l_sc[:, cols] + jnp.sum(pt, axis=0, keepdims=True)
            acc_sc[:, cols] = alpha * acc_sc[:, cols] + jnp.dot(vt, pt.astype(BF16),
                                                                preferred_element_type=F32)
            m_sc[:, cols] = m_new

    pl.when(j == i)(lambda: step(True))
    pl.when(j != i)(lambda: step(False))

    @pl.when(last_ref[p] == 1)
    def _():
        out_t = acc_sc[...] / l_sc[...]
        for g in range(group):
            o_ref[:, g * hd:(g + 1) * hd] = out_t[:, g * tile:(g + 1) * tile].T.astype(o_ref.dtype)


def _flash_prompt(q, qx, kb, kx, vt, kmt, *, batch, seq, n_heads, n_kv, hd, moba):
    tile = ATTN_TILE
    assert seq % tile == 0 and tile % MOBA_BLOCK == 0 and tile & (tile - 1) == 0
    nt = seq // tile
    group = n_heads // n_kv
    qi = np.concatenate([np.full(i + 1, i) for i in range(nt)]).astype(np.int32)
    kj = np.concatenate([np.array([i] + list(range(i))) for i in range(nt)]).astype(np.int32)
    last = np.concatenate([np.array([0] * i + [1]) for i in range(nt)]).astype(np.int32)
    gw = group * hd
    q_map = lambda b, h, p, qi, kj, la: (b * nt + qi[p], h)
    k_map = lambda b, h, p, qi, kj, la: (b * nt + kj[p], h)
    vt_spec = pl.BlockSpec((hd, tile), lambda b, h, p, qi, kj, la: (h, b * nt + kj[p]))
    in_specs = [pl.BlockSpec((tile, gw), q_map)]
    if moba:
        in_specs += [pl.BlockSpec((tile, group * LANE), lambda b, h, p, qi, kj, la: (qi[p], h)),
                     pl.BlockSpec((tile, hd), k_map),
                     pl.BlockSpec((tile, LANE), lambda b, h, p, qi, kj, la: (kj[p], 0)),
                     vt_spec,
                     pl.BlockSpec((None, None, hd, LANE), lambda b, h, p, qi, kj, la: (b, h, 0, 0))]
        ins = [q, qx, kb, kx, vt, kmt]
    else:
        in_specs += [pl.BlockSpec((tile, group * LANE), q_map),
                     pl.BlockSpec((tile, hd), k_map),
                     pl.BlockSpec((tile, LANE), k_map),
                     vt_spec]
        ins = [q, qx, kb, kx, vt]
    body = functools.partial(_flash_body, group=group, hd=hd, moba=moba, tile=tile)
    grid_spec = pltpu.PrefetchScalarGridSpec(
        num_scalar_prefetch=3, grid=(batch, n_kv, len(qi)), in_specs=in_specs,
        out_specs=pl.BlockSpec((tile, gw), q_map),
        scratch_shapes=[pltpu.VMEM((group * tile, hd + LANE), BF16),
                        pltpu.VMEM((1, group * tile), F32), pltpu.VMEM((1, group * tile), F32),
                        pltpu.VMEM((hd, group * tile), F32)])
    return pl.pallas_call(
        body, grid_spec=grid_spec, out_shape=jax.ShapeDtypeStruct((batch * seq, n_heads * hd), BF16),
        compiler_params=_cparams(("arbitrary", "arbitrary", "arbitrary")),
        name="moba_prompt" if moba else "fox_prompt")(jnp.asarray(qi), jnp.asarray(kj),
                                                      jnp.asarray(last), *ins)


def _moba_static_cols(seq, n_heads):
    pos = jnp.arange(seq, dtype=jnp.int32)
    slopes = jnp.exp2(-8.0 * jnp.arange(1, n_heads + 1, dtype=F32) / n_heads) * LOG2E
    qparts = _split3(-(slopes[None, :] * pos.astype(F32)[:, None]))
    sparts = _split3(slopes)
    qx = jnp.zeros((seq, n_heads, LANE), F32)
    for c in range(3):
        qx = qx.at[:, :, COL_QPOS + c].set(qparts[c])
        qx = qx.at[:, :, COL_KPOS + c].set(jnp.broadcast_to(sparts[c][None, :], (seq, n_heads)))
        qx = qx.at[:, :, COL_KPOS + 3 + c].set(jnp.broadcast_to(sparts[c][None, :], (seq, n_heads)))
    blk = pos // MOBA_BLOCK
    lane = jnp.arange(LANE, dtype=jnp.int32)[None, :]
    kx = jnp.where(lane == blk[:, None], 1.0, 0.0)
    kx = jnp.where((lane >= COL_QPOS) & (lane < COL_QPOS + 3), 1.0, kx)
    kx = jnp.where((lane >= COL_KPOS) & (lane < COL_KPOS + 3), (blk * MOBA_BLOCK).astype(F32)[:, None], kx)
    kx = jnp.where((lane >= COL_KPOS + 3) & (lane < COL_KPOS + 6), (pos % MOBA_BLOCK).astype(F32)[:, None], kx)
    return qx.reshape(seq, n_heads * LANE).astype(BF16), kx.astype(BF16)


def _page_specs(n, block, layer_arg, n_pages, pages_per_step):
    def spec(i):
        def imap(b, s, pt):
            return (layer_arg, pt[b * n_pages + s * pages_per_step + i]) + (0,) * (len(block) - 2)
        return pl.BlockSpec(block, imap)
    return [spec(i) for i in range(n)]


def _cum_sample_body(pt_ref, *refs, pg):
    lf_refs, new_ref, c_ref, cnew_ref, carry_ref = refs[:pg], refs[pg], refs[pg + 1], refs[pg + 2], refs[pg + 3]
    s = pl.program_id(1)

    @pl.when(s == 0)
    def _():
        carry_ref[...] = jnp.zeros_like(carry_ref)

    triu = (lax.broadcasted_iota(jnp.int32, (LANE, LANE), 0)
            <= lax.broadcasted_iota(jnp.int32, (LANE, LANE), 1)).astype(F32)
    carry = carry_ref[...]
    nh = carry.shape[0]
    local = jnp.dot(jnp.concatenate([r[...] for r in lf_refs], axis=0), triu,
                    preferred_element_type=F32, precision=lax.Precision.HIGHEST)
    for i in range(pg):
        loc = local[i * nh:(i + 1) * nh, :]
        c_ref[:, i * LANE:(i + 1) * LANE] = loc + carry
        carry = carry + jnp.broadcast_to(loc[:, LANE - 1:LANE], loc.shape)
    carry_ref[...] = carry

    @pl.when(s == pl.num_programs(1) - 1)
    def _():
        cnew_ref[...] = jnp.dot(new_ref[...], triu, preferred_element_type=F32,
                                precision=lax.Precision.HIGHEST) + carry


def _cum_sample(page_table, logf_t, lfnew_t, *, n_heads):
    db, n_pages = page_table.shape
    page = logf_t.shape[2]
    assert page == LANE
    pg = math.gcd(PAGES_PER_STEP, n_pages)
    in_specs = _page_specs(pg, (None, None, n_heads, page), 0, n_pages, pg)
    in_specs.append(pl.BlockSpec((None, n_heads, LANE), lambda b, s, pt: (b, 0, 0)))
    grid_spec = pltpu.PrefetchScalarGridSpec(
        num_scalar_prefetch=1, grid=(db, n_pages // pg), in_specs=in_specs,
        out_specs=[pl.BlockSpec((None, n_heads, pg * page), lambda b, s, pt: (b, 0, s)),
                   pl.BlockSpec((None, n_heads, LANE), lambda b, s, pt: (b, 0, 0))],
        scratch_shapes=[pltpu.VMEM((n_heads, LANE), F32)])
    return pl.pallas_call(
        functools.partial(_cum_sample_body, pg=pg), grid_spec=grid_spec,
        out_shape=[jax.ShapeDtypeStruct((db, n_heads, n_pages * page), F32),
                   jax.ShapeDtypeStruct((db, n_heads, LANE), F32)],
        compiler_params=_cparams(("arbitrary", "arbitrary")),
        name="fox_sample_cumsum")(page_table.reshape(-1), *([logf_t[None]] * pg), lfnew_t)


def _row_token(shape, n_heads, dec_seq):
    row = lax.broadcasted_iota(jnp.int32, shape, 0)
    tok = jnp.zeros(shape, jnp.int32)
    for t in range(1, dec_seq):
        tok = tok + (row >= t * n_heads).astype(jnp.int32)
    return tok


def _stage_pages(k_refs, v_refs, kb, vb, page, n_kv, want_ksum=False):
    hd = kb.shape[1] // n_kv
    ksums = []
    for i, (k_ref, v_ref) in enumerate(zip(k_refs, v_refs)):
        parts = []
        for h in range(n_kv):
            kf = k_ref[pl.ds(h, page, stride=n_kv), :]
            kb[i * page:(i + 1) * page, h * hd:(h + 1) * hd] = kf.astype(BF16)
            vb[i * page:(i + 1) * page, h * hd:(h + 1) * hd] = (
                v_ref[pl.ds(h, page, stride=n_kv), :].astype(BF16))
            if want_ksum:
                parts.append(jnp.sum(kf, axis=0, keepdims=True))
        if want_ksum:
            ksums.append(jnp.concatenate(parts, axis=1))
    return ksums


def _fox_sample_body(pt_ref, *refs, pg, page, dec_seq, n_heads, n_kv):
    k_refs, v_refs = refs[:pg], refs[pg:2 * pg]
    (q_ref, cq_ref, c_ref, cnew_ref, kn_ref, vn_ref, o_ref, kb, vb, m_sc, l_sc, acc_sc) = refs[2 * pg:]
    s = pl.program_id(1)

    @pl.when(s == 0)
    def _():
        m_sc[...] = jnp.full(m_sc.shape, -jnp.inf, F32)
        l_sc[...] = jnp.zeros(l_sc.shape, F32)
        acc_sc[...] = jnp.zeros(acc_sc.shape, F32)

    def update(logits, v):
        m_prev = m_sc[...]
        m_new = jnp.maximum(m_prev, jnp.max(logits, axis=1, keepdims=True))
        alpha = jnp.exp2(m_prev - m_new)
        pr = jnp.exp2(logits - m_new)
        l_sc[...] = alpha * l_sc[...] + jnp.sum(pr, axis=1, keepdims=True)
        acc_sc[...] = alpha * acc_sc[...] + jnp.dot(pr.astype(BF16), v, preferred_element_type=F32)
        m_sc[...] = m_new

    q = q_ref[...]
    cq = cq_ref[...]
    _stage_pages(k_refs, v_refs, kb, vb, page, n_kv)
    ck = jnp.concatenate([c_ref[...]] * dec_seq, axis=0)
    update(_dot_nt(q, kb[...]) + (cq - ck) * LOG2E, vb[...])

    @pl.when(s == pl.num_programs(1) - 1)
    def _():
        rows = q.shape[0]
        ckn = jnp.concatenate([cnew_ref[...]] * dec_seq, axis=0)
        logits = _dot_nt(q, kn_ref[...].astype(BF16)) + (cq - ckn) * LOG2E
        tok = _row_token((rows, LANE), n_heads, dec_seq)
        logits = jnp.where(lax.broadcasted_iota(jnp.int32, (rows, LANE), 1) <= tok, logits, -jnp.inf)
        update(logits, vn_ref[...].astype(BF16))
        o_ref[...] = acc_sc[...] / l_sc[...]


def _moba_sample_body(pt_ref, *refs, pg, page, dec_seq, n_heads, n_kv, past):
    k_refs, v_refs = refs[:pg], refs[pg:2 * pg]
    (q_ref, q32_ref, slope_ref, kn_ref, vn_ref, o_ref,
     kb, vb, acc_all, m_all, l_all, km) = refs[2 * pg:]
    s = pl.program_id(1)
    nblk = past // MOBA_BLOCK
    ppb = MOBA_BLOCK // page
    rows = q_ref.shape[0]
    lane = lax.broadcasted_iota(jnp.int32, (rows, LANE), 1)
    qpos = (past + _row_token((rows, 1), n_heads, dec_seq)).astype(F32)
    km_row = lax.broadcasted_iota(jnp.int32, km.shape, 0)

    @pl.when(s == 0)
    def _():
        m_all[...] = jnp.full(m_all.shape, -jnp.inf, F32)
        l_all[...] = jnp.zeros(l_all.shape, F32)
        km[...] = jnp.zeros(km.shape, F32)

    def block_partial(logits, v, blk):
        mj = jnp.max(logits, axis=1, keepdims=True)
        pr = jnp.exp2(logits - mj)
        acc_all[blk] = jnp.dot(pr.astype(BF16), v, preferred_element_type=F32)
        m_all[...] = jnp.where(lane == blk, mj, m_all[...])
        l_all[...] = jnp.where(lane == blk, jnp.sum(pr, axis=1, keepdims=True), l_all[...])

    q = q_ref[...]
    slope = slope_ref[...]
    ksums = _stage_pages(k_refs, v_refs, kb, vb, page, n_kv, want_ksum=True)
    width = pg * page
    kpos = (s * width + lax.broadcasted_iota(jnp.int32, (1, width), 1)).astype(F32)
    logits = _dot_nt(q, kb[...]) - slope * (qpos - kpos)
    for jb in range(pg // ppb):
        blk = s * (pg // ppb) + jb
        block_partial(logits[:, jb * MOBA_BLOCK:(jb + 1) * MOBA_BLOCK],
                      vb[jb * MOBA_BLOCK:(jb + 1) * MOBA_BLOCK, :], blk)
        ksum = ksums[jb * ppb]
        for i in range(1, ppb):
            ksum = ksum + ksums[jb * ppb + i]
        km[...] = jnp.where(km_row == blk, ksum * (1.0 / MOBA_BLOCK), km[...])

    @pl.when(s == pl.num_programs(1) - 1)
    def _():
        npos = (past + lax.broadcasted_iota(jnp.int32, (1, LANE), 1)).astype(F32)
        ln = _dot_nt(q, kn_ref[...].astype(BF16)) - slope * (qpos - npos)
        ln = jnp.where(npos <= qpos, ln, -jnp.inf)
        block_partial(ln, vn_ref[...].astype(BF16), nblk)
        gate = lax.dot_general(q32_ref[...], km[...], (((1,), (1,)), ((), ())),
                               preferred_element_type=F32, precision=lax.Precision.HIGHEST)
        allowed = _top3_allowed(gate, lane, jnp.full((rows, LANE), nblk, jnp.int32)) > 0.0
        m_row = jnp.max(jnp.where(allowed, m_all[...], -jnp.inf), axis=1, keepdims=True)
        w = jnp.where(allowed, jnp.exp2(m_all[...] - m_row), 0.0)
        denom = jnp.sum(w * l_all[...], axis=1, keepdims=True)
        out = jnp.zeros(o_ref.shape, F32)
        for jb in range(nblk + 1):
            out = out + w[:, jb:jb + 1] * acc_all[jb]
        o_ref[...] = out / denom


def _sample_attention(page_table, cache_k4, cache_v4, layer, qbd, kn_pad, vn_pad, extra, *, moba,
                      dec_seq, n_heads):
    db, n_pages = page_table.shape
    kd = qbd.shape[2]
    hd = cache_k4.shape[3]
    n_kv = kd // hd
    page = cache_k4.shape[2] // n_kv
    rows = qbd.shape[1]
    past = n_pages * page
    pg = math.gcd(PAGES_PER_STEP, n_pages)
    assert page == LANE and MOBA_BLOCK % page == 0 and pg % (MOBA_BLOCK // page) == 0
    assert past % MOBA_BLOCK == 0 and past // MOBA_BLOCK < LANE
    page_specs = (_page_specs(pg, (None, None, page * n_kv, hd), layer, n_pages, pg)
                  + _page_specs(pg, (None, None, page * n_kv, hd), layer, n_pages, pg))
    per_b = lambda a: pl.BlockSpec((None,) + a.shape[1:], lambda b, s, pt: (b,) + (0,) * (a.ndim - 1))
    common = [pltpu.VMEM((pg * page, kd), BF16), pltpu.VMEM((pg * page, kd), BF16)]
    if moba:
        q32, slope = extra
        ins = [qbd, q32, slope, kn_pad, vn_pad]
        in_specs = page_specs + [per_b(qbd), per_b(q32),
                                 pl.BlockSpec(slope.shape, lambda b, s, pt: (0, 0)),
                                 per_b(kn_pad), per_b(vn_pad)]
        scratch = common + [pltpu.VMEM((past // MOBA_BLOCK + 1, rows, kd), F32),
                            pltpu.VMEM((rows, LANE), F32), pltpu.VMEM((rows, LANE), F32),
                            pltpu.VMEM((LANE, kd), F32)]
        body = functools.partial(_moba_sample_body, pg=pg, page=page, dec_seq=dec_seq,
                                 n_heads=n_heads, n_kv=n_kv, past=past)
    else:
        cq, c_t, cnew_t = extra
        ins = [qbd, cq, c_t, cnew_t, kn_pad, vn_pad]
        in_specs = page_specs + [per_b(qbd), per_b(cq),
                                 pl.BlockSpec((None, n_heads, pg * page), lambda b, s, pt: (b, 0, s)),
                                 per_b(cnew_t), per_b(kn_pad), per_b(vn_pad)]
        scratch = common + [pltpu.VMEM((rows, 1), F32), pltpu.VMEM((rows, 1), F32),
                            pltpu.VMEM((rows, kd), F32)]
        body = functools.partial(_fox_sample_body, pg=pg, page=page, dec_seq=dec_seq, n_heads=n_heads,
                                 n_kv=n_kv)
    grid_spec = pltpu.PrefetchScalarGridSpec(
        num_scalar_prefetch=1, grid=(db, n_pages // pg), in_specs=in_specs,
        out_specs=pl.BlockSpec((None, rows, kd), lambda b, s, pt: (b, 0, 0)), scratch_shapes=scratch)
    return pl.pallas_call(
        body, grid_spec=grid_spec, out_shape=jax.ShapeDtypeStruct((db, rows, kd), F32),
        compiler_params=_cparams(("arbitrary", "arbitrary")),
        name="moba_sample" if moba else "fox_sample")(
            page_table.reshape(-1), *([cache_k4] * pg), *([cache_v4] * pg), *ins)


def _wo_mlp_body(x_ref, mix_ref, wo_ref, g_ref, wu_ref, wd_ref, o_ref, hn_sc):
    f = pl.program_id(1)

    @pl.when(f == 0)
    def _():
        h = x_ref[...] + jnp.dot(mix_ref[...], wo_ref[...], preferred_element_type=F32)
        ms = jnp.mean(h * h, axis=-1, keepdims=True)
        hn_sc[...] = (h * lax.rsqrt(ms + RMS_EPS) * g_ref[...]).astype(BF16)
        o_ref[...] = h

    u = jnp.maximum(jnp.dot(hn_sc[...], wu_ref[...], preferred_element_type=F32), 0.0)
    o_ref[...] += jnp.dot((u * u).astype(BF16), wd_ref[...], preferred_element_type=F32)


def _wo_mlp_residual(x, mix_bf, wo_bf, g_mlp, wu_bf, wd_bf, *, tm, tf):
    rows, d = x.shape
    dff = wu_bf.shape[1]
    tm = _row_tile(rows, tm)
    tf = min(tf, dff)
    assert dff % tf == 0
    return pl.pallas_call(
        _wo_mlp_body, grid=(rows // tm, dff // tf),
        in_specs=[pl.BlockSpec((tm, d), lambda i, f: (i, 0)),
                  pl.BlockSpec((tm, mix_bf.shape[1]), lambda i, f: (i, 0)),
                  pl.BlockSpec(wo_bf.shape, lambda i, f: (0, 0)),
                  pl.BlockSpec(g_mlp.shape, lambda i, f: (0, 0)),
                  pl.BlockSpec((d, tf), lambda i, f: (0, f)),
                  pl.BlockSpec((tf, d), lambda i, f: (f, 0))],
        out_specs=pl.BlockSpec((tm, d), lambda i, f: (i, 0)),
        out_shape=jax.ShapeDtypeStruct((rows, d), F32),
        scratch_shapes=[pltpu.VMEM((tm, d), BF16)],
        compiler_params=_cparams(("arbitrary", "arbitrary")),
        name="wo_mlp_residual")(x, mix_bf, wo_bf, g_mlp, wu_bf, wd_bf)


def kernel(x_prompt, x_sample, cache_k, cache_v, cache_logf, page_table, attn_norm, w_qkv, q_norm, k_norm,
           w_fgate, b_fgate, w_o, mlp_norm, w_up, w_down):
    batch, seq, d = x_prompt.shape
    db, dec_seq, _ = x_sample.shape
    depth, n_pool, page, n_kv, hd = cache_k.shape
    n_heads = w_fgate.shape[2]
    group = n_heads // n_kv
    qd, kd = n_heads * hd, n_kv * hd
    n_pages = page_table.shape[1]
    past = n_pages * page
    rows_s = dec_seq * n_heads

    hp = x_prompt.reshape(batch * seq, d)
    hs = x_sample.reshape(db * dec_seq, d)
    cache_k4 = cache_k.reshape(depth, n_pool, page * n_kv, hd)
    cache_v4 = cache_v.reshape(depth, n_pool, page * n_kv, hd)
    row_kv = (jnp.arange(rows_s) % n_heads) // group
    row_mask = (row_kv[:, None] == jnp.arange(n_kv)[None, :]).astype(F32)
    slopes = jnp.exp2(-8.0 * jnp.arange(1, n_heads + 1, dtype=F32) / n_heads)

    def block_diag(q2d, dtype):
        q = q2d.astype(F32).reshape(db, rows_s, 1, hd) * row_mask[None, :, :, None]
        return q.reshape(db, rows_s, kd).astype(dtype)

    def pad_page(a2d):
        a = a2d.reshape(db, dec_seq, kd)
        return jnp.pad(a, ((0, 0), (0, LANE - dec_seq), (0, 0)))

    def own_lanes(o):
        o6 = o.reshape(db, dec_seq, n_kv, group, n_kv, hd)
        return jnp.stack([o6[:, :, k, :, k, :] for k in range(n_kv)], axis=2).reshape(db * dec_seq, qd)

    outs = {k: [] for k in ("kp", "vp", "fp", "ks", "vs", "fs")}
    for i in range(depth):
        fox = i % 2 == 0
        j = i // 2
        w_bf = w_qkv[i].astype(BF16)
        g_attn, qg, kg = attn_norm[i][None, :], q_norm[i][None, :], k_norm[i][None, :]
        if fox:
            wf_bf = jnp.pad(w_fgate[j], ((0, 0), (0, LANE - n_heads))).astype(BF16)
            bf_pad = jnp.pad(b_fgate[j], (0, LANE - n_heads))[None, :]
        else:
            wf_bf = bf_pad = None
        proj = functools.partial(_qkv_proj, g_attn=g_attn, w_bf=w_bf, q_gain=qg, k_gain=kg, wf_bf=wf_bf,
                                 bf_pad=bf_pad, n_heads=n_heads, n_kv=n_kv, hd=hd, q_f32=not fox)
        res_p = proj(hp, attn_copies=True, tm=512)
        res_s = proj(hs, attn_copies=False, tm=128)
        qp, kp32, vp32, kpb, vpt = res_p[:5]
        qs, ks32, vs32 = res_s[:3]
        outs["kp"].append(kp32.reshape(batch, seq, n_kv, hd))
        outs["vp"].append(vp32.reshape(batch, seq, n_kv, hd))
        outs["ks"].append(ks32.reshape(db, dec_seq, n_kv, hd))
        outs["vs"].append(vs32.reshape(db, dec_seq, n_kv, hd))
        kn_pad, vn_pad = pad_page(ks32), pad_page(vs32)
        if fox:
            lfp, lfp16, lfs16 = res_p[5], res_p[6], res_s[4]
            outs["fp"].append(lfp16.reshape(batch, seq, n_heads))
            outs["fs"].append(lfs16.reshape(db, dec_seq, n_heads))
            qx, kx = _fox_aug(lfp, batch=batch, seq=seq, n_heads=n_heads, n_kv=n_kv)
            mix_p = _flash_prompt(qp, qx, kpb, kx, vpt, None, batch=batch, seq=seq, n_heads=n_heads,
                                  n_kv=n_kv, hd=hd, moba=False)
            logf_t = jnp.swapaxes(cache_logf[j], 1, 2)
            lfnew_t = jnp.pad(jnp.swapaxes(lfs16.reshape(db, dec_seq, n_heads), 1, 2),
                              ((0, 0), (0, 0), (0, LANE - dec_seq)))
            c_t, cnew_t = _cum_sample(page_table, logf_t, lfnew_t, n_heads=n_heads)
            cq = jnp.swapaxes(cnew_t[:, :, :dec_seq], 1, 2).reshape(db, rows_s, 1)
            o_s = _sample_attention(page_table, cache_k4, cache_v4, i, block_diag(qs, BF16), kn_pad, vn_pad,
                                    (cq, c_t, cnew_t), moba=False, dec_seq=dec_seq, n_heads=n_heads)
        else:
            kmean = _kmean(kp32, batch=batch, seq=seq, n_kv=n_kv)
            kmt = jnp.swapaxes(kmean.reshape(batch, LANE, n_kv, hd), 1, 3)
            kmt = jnp.swapaxes(kmt, 1, 2)
            qx, kx = _moba_static_cols(seq, n_heads)
            mix_p = _flash_prompt(qp, qx, kpb, kx, vpt, kmt, batch=batch, seq=seq, n_heads=n_heads,
                                  n_kv=n_kv, hd=hd, moba=True)
            slope_col = (jnp.tile(slopes, dec_seq) * LOG2E)[:, None]
            q_scaled = qs * (hd ** -0.5 * LOG2E)
            o_s = _sample_attention(page_table, cache_k4, cache_v4, i, block_diag(q_scaled, BF16), kn_pad,
                                    vn_pad, (block_diag(qs, F32), slope_col), moba=True, dec_seq=dec_seq,
                                    n_heads=n_heads)
        mix_s = own_lanes(o_s).astype(BF16)
        wo_bf = w_o[i].astype(BF16)
        wu_bf, wd_bf = w_up[i].astype(BF16), w_down[i].astype(BF16)
        g_mlp = mlp_norm[i][None, :]
        hp = _wo_mlp_residual(hp, mix_p, wo_bf, g_mlp, wu_bf, wd_bf, tm=512, tf=512)
        hs = _wo_mlp_residual(hs, mix_s, wo_bf, g_mlp, wu_bf, wd_bf, tm=128, tf=512)
    return (hp.reshape(batch, seq, d), hs.reshape(db, dec_seq, d),
            jnp.stack(outs["kp"]), jnp.stack(outs["vp"]), jnp.stack(outs["fp"]),
            jnp.stack(outs["ks"]), jnp.stack(outs["vs"]), jnp.stack(outs["fs"]))
```

```python
import functools
import math

import numpy as np
import jax
import jax.numpy as jnp
from jax import lax
from jax.experimental import pallas as pl
from jax.experimental.pallas import tpu as pltpu

F32 = jnp.float32
BF16 = jnp.bfloat16
LOG2E = 1.4426950408889634
RMS_EPS = 1e-6
MOBA_BLOCK = 256
MOBA_TOPK = 3
LANE = 128
ATTN_TILE = 512
FLASH_CHUNK = 512
DENOM_ROWS = 16
PAGES_PER_STEP = 32
NEG_BIG = -(2.0 ** 100)
VMEM_LIMIT = 56 * 1024 * 1024
COL_BLK = 0
COL_QPOS = 64
COL_KPOS = 67


def _cparams(sem):
    return pltpu.CompilerParams(dimension_semantics=sem, vmem_limit_bytes=VMEM_LIMIT)


def _row_tile(rows, pref):
    t = min(rows, pref)
    while rows % t:
        t -= 8
    return t


def _split3(x):
    hi = x.astype(BF16).astype(F32)
    r1 = x - hi
    mid = r1.astype(BF16).astype(F32)
    lo = (r1 - mid).astype(BF16).astype(F32)
    return hi, mid, lo


def _dot_nt(a, b):
    return lax.dot_general(a, b, (((1,), (1,)), ((), ())), preferred_element_type=F32)


def _qkv_body(*refs, n_heads, n_kv, hd, q_scale, with_gate, q_f32, attn_copies):
    n_in = 7 if with_gate else 5
    x_ref, g_ref, w_ref, qg_ref, kg_ref = refs[:5]
    outs = list(refs[n_in:])
    q_ref, k32_ref, v32_ref = outs[:3]
    del outs[:3]
    if attn_copies:
        kb_ref, vt_ref = outs[:2]
        del outs[:2]
    qd, kd = n_heads * hd, n_kv * hd
    x = x_ref[...]
    ms = jnp.mean(x * x, axis=-1, keepdims=True)
    xn = (x * lax.rsqrt(ms + RMS_EPS) * g_ref[...]).astype(BF16)

    def head_norm(a, gain):
        m = jnp.mean(a * a, axis=-1, keepdims=True)
        return a * lax.rsqrt(m + RMS_EPS) * gain

    chunk = 4 * hd
    for c in range(qd // chunk):
        acc = jnp.dot(xn, w_ref[:, c * chunk:(c + 1) * chunk], preferred_element_type=F32)
        for j in range(4):
            qn = head_norm(acc[:, j * hd:(j + 1) * hd], qg_ref[...])
            lo = c * chunk + j * hd
            if q_f32:
                q_ref[:, lo:lo + hd] = qn
            else:
                q_ref[:, lo:lo + hd] = (qn * q_scale).astype(BF16)
    tm = x.shape[0]
    acc = jnp.dot(xn, w_ref[:, qd:qd + kd], preferred_element_type=F32)
    for j in range(n_kv):
        kn = head_norm(acc[:, j * hd:(j + 1) * hd], kg_ref[...])
        k32_ref[pl.ds(j, tm, stride=n_kv), :] = kn
        if attn_copies:
            kb_ref[:, j * hd:(j + 1) * hd] = kn.astype(BF16)
    acc = jnp.dot(xn, w_ref[:, qd + kd:qd + 2 * kd], preferred_element_type=F32)
    for j in range(n_kv):
        v32_ref[pl.ds(j, tm, stride=n_kv), :] = acc[:, j * hd:(j + 1) * hd]
    if attn_copies:
        vt_ref[...] = acc.T.astype(BF16)
    if with_gate:
        wf_ref, bf_ref = refs[5:7]
        z = jnp.dot(xn, wf_ref[...], preferred_element_type=F32) + bf_ref[...]
        lf = jnp.minimum(z, 0.0) - jnp.log1p(jnp.exp(-jnp.abs(z)))
        outs[0][...] = lf
        outs[1][...] = lf[:, :n_heads]


def _qkv_proj(x, g_attn, w_bf, q_gain, k_gain, wf_bf, bf_pad, *, layer, n_heads, n_kv, hd, q_f32,
              attn_copies, tm):
    rows, d = x.shape
    qd, kd = n_heads * hd, n_kv * hd
    with_gate = wf_bf is not None
    tm = _row_tile(rows, tm)
    row_spec = lambda w: pl.BlockSpec((tm, w), lambda i: (i, 0))
    full = lambda a: pl.BlockSpec(a.shape, lambda i: (0,) * a.ndim)
    ins = [x, g_attn, w_bf, q_gain, k_gain]
    in_specs = [row_spec(d), full(g_attn), pl.BlockSpec((None,) + w_bf.shape[1:], lambda i: (layer, 0, 0)),
                full(q_gain), full(k_gain)]
    kv_spec = pl.BlockSpec((tm * n_kv, hd), lambda i: (i, 0))
    out_shape = [jax.ShapeDtypeStruct((rows, qd), F32 if q_f32 else BF16),
                 jax.ShapeDtypeStruct((rows * n_kv, hd), F32), jax.ShapeDtypeStruct((rows * n_kv, hd), F32)]
    out_specs = [row_spec(qd), kv_spec, kv_spec]
    if attn_copies:
        out_shape += [jax.ShapeDtypeStruct((rows, kd), BF16), jax.ShapeDtypeStruct((kd, rows), BF16)]
        out_specs += [row_spec(kd), pl.BlockSpec((kd, tm), lambda i: (0, i))]
    if with_gate:
        ins += [wf_bf, bf_pad]
        in_specs += [full(wf_bf), full(bf_pad)]
        out_shape += [jax.ShapeDtypeStruct((rows, LANE), F32), jax.ShapeDtypeStruct((rows, n_heads), F32)]
        out_specs += [row_spec(LANE), row_spec(n_heads)]
    body = functools.partial(_qkv_body, n_heads=n_heads, n_kv=n_kv, hd=hd,
                             q_scale=hd ** -0.5 * LOG2E, with_gate=with_gate, q_f32=q_f32,
                             attn_copies=attn_copies)
    return pl.pallas_call(body, grid=(rows // tm,), in_specs=in_specs, out_specs=out_specs,
                          out_shape=out_shape, compiler_params=_cparams(("arbitrary",)),
                          name="qkv_proj")(*ins)


def _fox_aug_body(lf_ref, qx_ref, kx_ref, carry_ref, *, n_heads, n_kv):
    t = pl.program_id(1)

    @pl.when(t == 0)
    def _():
        carry_ref[...] = jnp.zeros_like(carry_ref)

    lf = lf_ref[...]
    tc = lf.shape[0]
    tri = (lax.broadcasted_iota(jnp.int32, (tc, tc), 1)
           <= lax.broadcasted_iota(jnp.int32, (tc, tc), 0)).astype(F32)
    c = jnp.dot(tri, lf, preferred_element_type=F32, precision=lax.Precision.HIGHEST) + carry_ref[...]
    carry_ref[...] = c[tc - 1:tc, :]
    c2 = c * LOG2E
    lane = lax.broadcasted_iota(jnp.int32, (tc, LANE), 1)
    group = n_heads // n_kv
    ones3 = jnp.where(lane < 3, 1.0, 0.0)
    for kvh in range(n_kv):
        kaug = ones3
        for g in range(group):
            h = kvh * group + g
            hi, mid, lo = _split3(jnp.broadcast_to(c2[:, h:h + 1], (tc, LANE)))
            base = 3 + 3 * g
            own = jnp.where((lane >= base) & (lane < base + 3), 1.0, 0.0)
            qaug = jnp.where(lane == 0, hi, jnp.where(lane == 1, mid, jnp.where(lane == 2, lo, own)))
            qx_ref[:, h * LANE:(h + 1) * LANE] = qaug.astype(BF16)
            kaug = jnp.where(lane == base, -hi,
                             jnp.where(lane == base + 1, -mid, jnp.where(lane == base + 2, -lo, kaug)))
        kx_ref[:, kvh * LANE:(kvh + 1) * LANE] = kaug.astype(BF16)


def _fox_aug(lf128, *, batch, seq, n_heads, n_kv):
    tc = _row_tile(seq, 512)
    nt = seq // tc
    body = functools.partial(_fox_aug_body, n_heads=n_heads, n_kv=n_kv)
    return pl.pallas_call(
        body, grid=(batch, nt),
        in_specs=[pl.BlockSpec((tc, LANE), lambda b, t: (b * nt + t, 0))],
        out_specs=[pl.BlockSpec((tc, n_heads * LANE), lambda b, t: (b * nt + t, 0)),
                   pl.BlockSpec((tc, n_kv * LANE), lambda b, t: (b * nt + t, 0))],
        out_shape=[jax.ShapeDtypeStruct((batch * seq, n_heads * LANE), BF16),
                   jax.ShapeDtypeStruct((batch * seq, n_kv * LANE), BF16)],
        scratch_shapes=[pltpu.VMEM((1, LANE), F32)],
        compiler_params=_cparams(("arbitrary", "arbitrary")), name="fox_aug")(lf128)


def _kmean_body(k_ref, o_ref, *, nblk, n_kv):
    hd = k_ref.shape[1]
    for j in range(nblk):
        for h in range(n_kv):
            blk = k_ref[pl.ds(j * MOBA_BLOCK * n_kv + h, MOBA_BLOCK, stride=n_kv), :]
            o_ref[j:j + 1, h * hd:(h + 1) * hd] = jnp.sum(blk, axis=0, keepdims=True) * (1.0 / MOBA_BLOCK)
    if o_ref.shape[0] > nblk:
        o_ref[nblk:, :] = jnp.zeros((o_ref.shape[0] - nblk, o_ref.shape[1]), F32)


def _kmean(k32, *, batch, seq, n_kv):
    hd = k32.shape[1]
    nblk = seq // MOBA_BLOCK
    assert nblk <= COL_QPOS
    return pl.pallas_call(
        functools.partial(_kmean_body, nblk=nblk, n_kv=n_kv), grid=(batch,),
        in_specs=[pl.BlockSpec((seq * n_kv, hd), lambda b: (b, 0))],
        out_specs=pl.BlockSpec((None, LANE, n_kv * hd), lambda b: (b, 0, 0)),
        out_shape=jax.ShapeDtypeStruct((batch, LANE, n_kv * hd), F32),
        compiler_params=_cparams(("arbitrary",)), name="moba_kmean")(k32)


def _top3_allowed(gate, lane, own):
    lane_f = lane.astype(F32)
    gm = jnp.where(lane < own, gate, -jnp.inf)
    sel = jnp.zeros(gate.shape, F32)
    for _ in range(MOBA_TOPK):
        mx = jnp.max(gm, axis=1, keepdims=True)
        idx = jnp.min(jnp.where(gm == mx, lane_f, float(LANE)), axis=1, keepdims=True)
        pick = lane_f == idx
        sel = jnp.where(pick, 1.0, sel)
        gm = jnp.where(pick, -jnp.inf, gm)
    return jnp.where(lane < own, sel, jnp.where(lane == own, 1.0, 0.0))


def _flash_body(qi_ref, kj_ref, last_ref, *refs, group, hd, moba, tile):
    if moba:
        q_ref, qx_ref, k_ref, kx_ref, vt_ref, kmt_ref, o_ref, qaug, m_sc, acc_sc = refs
    else:
        q_ref, qx_ref, k_ref, kx_ref, vt_ref, o_ref, qaug, m_sc, acc_sc = refs
    p = pl.program_id(2)
    i, j = qi_ref[p], kj_ref[p]

    @pl.when(j == i)
    def _():
        for g in range(group):
            rows = slice(g * tile, (g + 1) * tile)
            if moba:
                qf = q_ref[:, g * hd:(g + 1) * hd]
                gate = jnp.dot(qf, kmt_ref[...], preferred_element_type=F32,
                               precision=lax.Precision.HIGHEST)
                lane = lax.broadcasted_iota(jnp.int32, (tile, LANE), 1)
                pos = i * tile + lax.broadcasted_iota(jnp.int32, (tile, LANE), 0)
                own = lax.shift_right_logical(pos, int(math.log2(MOBA_BLOCK)))
                allowed = _top3_allowed(gate, lane, own)
                bias = jnp.where(allowed > 0.0, 0.0, NEG_BIG)
                aug = jnp.where(lane < COL_QPOS, bias, qx_ref[:, g * LANE:(g + 1) * LANE].astype(F32))
                qaug[rows, :hd] = (qf * (hd ** -0.5 * LOG2E)).astype(BF16)
                qaug[rows, hd:] = aug.astype(BF16)
            else:
                qaug[rows, :hd] = q_ref[:, g * hd:(g + 1) * hd]
                qaug[rows, hd:] = qx_ref[:, g * LANE:(g + 1) * LANE]
        m_sc[...] = jnp.full(m_sc.shape, -jnp.inf, F32)
        acc_sc[...] = jnp.zeros(acc_sc.shape, F32)

    def step(diagonal):
        kaug = jnp.concatenate([k_ref[...], kx_ref[...]], axis=1)
        vt = jnp.concatenate([vt_ref[...], jnp.ones((DENOM_ROWS, tile), BF16)], axis=0)
        n_chunks = group * tile // FLASH_CHUNK
        scores = [_dot_nt(kaug, qaug[0:FLASH_CHUNK, :])]
        for c in range(n_chunks):
            cols = slice(c * FLASH_CHUNK, (c + 1) * FLASH_CHUNK)
            if c + 1 < n_chunks:
                scores.append(_dot_nt(kaug, qaug[(c + 1) * FLASH_CHUNK:(c + 2) * FLASH_CHUNK, :]))
            st = scores[c]
            if diagonal:
                key = lax.broadcasted_iota(jnp.int32, st.shape, 0)
                query = (c * FLASH_CHUNK) % tile + lax.broadcasted_iota(jnp.int32, st.shape, 1)
                st = jnp.where(key <= query, st, -jnp.inf)
            m_prev = m_sc[:, cols]
            m_new = jnp.maximum(m_prev, jnp.max(st, axis=0, keepdims=True))
            alpha = jnp.exp2(m_prev - m_new)
            pt = jnp.exp2(st - m_new)
            acc_sc[:, cols] = alpha * acc_sc[:, cols] + jnp.dot(vt, pt.astype(BF16),
                                                                preferred_element_type=F32)
            m_sc[:, cols] = m_new

    pl.when(j == i)(lambda: step(True))
    pl.when(j != i)(lambda: step(False))

    @pl.when(last_ref[p] == 1)
    def _():
        out_t = acc_sc[:hd, :] / acc_sc[hd:hd + 1, :]
        for g in range(group):
            o_ref[:, g * hd:(g + 1) * hd] = out_t[:, g * tile:(g + 1) * tile].T.astype(o_ref.dtype)


def _flash_prompt(q, qx, kb, kx, vt, kmt, *, batch, seq, n_heads, n_kv, hd, moba):
    tile = ATTN_TILE
    assert seq % tile == 0 and tile % MOBA_BLOCK == 0 and tile & (tile - 1) == 0
    nt = seq // tile
    group = n_heads // n_kv
    qi = np.concatenate([np.full(i + 1, i) for i in range(nt)]).astype(np.int32)
    kj = np.concatenate([np.array([i] + list(range(i))) for i in range(nt)]).astype(np.int32)
    last = np.concatenate([np.array([0] * i + [1]) for i in range(nt)]).astype(np.int32)
    gw = group * hd
    q_map = lambda b, h, p, qi, kj, la: (b * nt + qi[p], h)
    k_map = lambda b, h, p, qi, kj, la: (b * nt + kj[p], h)
    vt_spec = pl.BlockSpec((hd, tile), lambda b, h, p, qi, kj, la: (h, b * nt + kj[p]))
    in_specs = [pl.BlockSpec((tile, gw), q_map)]
    if moba:
        in_specs += [pl.BlockSpec((tile, group * LANE), lambda b, h, p, qi, kj, la: (qi[p], h)),
                     pl.BlockSpec((tile, hd), k_map),
                     pl.BlockSpec((tile, LANE), lambda b, h, p, qi, kj, la: (kj[p], 0)),
                     vt_spec,
                     pl.BlockSpec((None, None, hd, LANE), lambda b, h, p, qi, kj, la: (b, h, 0, 0))]
        ins = [q, qx, kb, kx, vt, kmt]
    else:
        in_specs += [pl.BlockSpec((tile, group * LANE), q_map),
                     pl.BlockSpec((tile, hd), k_map),
                     pl.BlockSpec((tile, LANE), k_map),
                     vt_spec]
        ins = [q, qx, kb, kx, vt]
    body = functools.partial(_flash_body, group=group, hd=hd, moba=moba, tile=tile)
    grid_spec = pltpu.PrefetchScalarGridSpec(
        num_scalar_prefetch=3, grid=(batch, n_kv, len(qi)), in_specs=in_specs,
        out_specs=pl.BlockSpec((tile, gw), q_map),
        scratch_shapes=[pltpu.VMEM((group * tile, hd + LANE), BF16),
                        pltpu.VMEM((1, group * tile), F32),
                        pltpu.VMEM((hd + DENOM_ROWS, group * tile), F32)])
    return pl.pallas_call(
        body, grid_spec=grid_spec, out_shape=jax.ShapeDtypeStruct((batch * seq, n_heads * hd), BF16),
        compiler_params=_cparams(("arbitrary", "arbitrary", "arbitrary")),
        name="moba_prompt" if moba else "fox_prompt")(jnp.asarray(qi), jnp.asarray(kj),
                                                      jnp.asarray(last), *ins)


def _moba_static_cols(seq, n_heads):
    pos = jnp.arange(seq, dtype=jnp.int32)
    slopes = jnp.exp2(-8.0 * jnp.arange(1, n_heads + 1, dtype=F32) / n_heads) * LOG2E
    qparts = _split3(-(slopes[None, :] * pos.astype(F32)[:, None]))
    sparts = _split3(slopes)
    qx = jnp.zeros((seq, n_heads, LANE), F32)
    for c in range(3):
        qx = qx.at[:, :, COL_QPOS + c].set(qparts[c])
        qx = qx.at[:, :, COL_KPOS + c].set(jnp.broadcast_to(sparts[c][None, :], (seq, n_heads)))
        qx = qx.at[:, :, COL_KPOS + 3 + c].set(jnp.broadcast_to(sparts[c][None, :], (seq, n_heads)))
    blk = pos // MOBA_BLOCK
    lane = jnp.arange(LANE, dtype=jnp.int32)[None, :]
    kx = jnp.where(lane == blk[:, None], 1.0, 0.0)
    kx = jnp.where((lane >= COL_QPOS) & (lane < COL_QPOS + 3), 1.0, kx)
    kx = jnp.where((lane >= COL_KPOS) & (lane < COL_KPOS + 3), (blk * MOBA_BLOCK).astype(F32)[:, None], kx)
    kx = jnp.where((lane >= COL_KPOS + 3) & (lane < COL_KPOS + 6), (pos % MOBA_BLOCK).astype(F32)[:, None], kx)
    return qx.reshape(seq, n_heads * LANE).astype(BF16), kx.astype(BF16)


def _page_specs(n, block, layer_arg, n_pages, pages_per_step):
    def spec(i):
        def imap(b, s, pt):
            return (layer_arg, pt[b * n_pages + s * pages_per_step + i]) + (0,) * (len(block) - 2)
        return pl.BlockSpec(block, imap)
    return [spec(i) for i in range(n)]


def _cum_sample_body(pt_ref, *refs, pg):
    lf_refs, new_ref, c_ref, cnew_ref, carry_ref = refs[:pg], refs[pg], refs[pg + 1], refs[pg + 2], refs[pg + 3]
    s = pl.program_id(1)

    @pl.when(s == 0)
    def _():
        carry_ref[...] = jnp.zeros_like(carry_ref)

    triu = (lax.broadcasted_iota(jnp.int32, (LANE, LANE), 0)
            <= lax.broadcasted_iota(jnp.int32, (LANE, LANE), 1)).astype(F32)
    carry = carry_ref[...]
    nh = carry.shape[0]
    local = jnp.dot(jnp.concatenate([r[...] for r in lf_refs], axis=0), triu,
                    preferred_element_type=F32, precision=lax.Precision.HIGHEST)
    for i in range(pg):
        loc = local[i * nh:(i + 1) * nh, :]
        c_ref[:, i * LANE:(i + 1) * LANE] = loc + carry
        carry = carry + jnp.broadcast_to(loc[:, LANE - 1:LANE], loc.shape)
    carry_ref[...] = carry

    @pl.when(s == pl.num_programs(1) - 1)
    def _():
        cnew_ref[...] = jnp.dot(new_ref[...], triu, preferred_element_type=F32,
                                precision=lax.Precision.HIGHEST) + carry


def _cum_sample(page_table, logf_t, lfnew_t, *, n_heads):
    db, n_pages = page_table.shape
    page = logf_t.shape[2]
    assert page == LANE
    pg = math.gcd(PAGES_PER_STEP, n_pages)
    in_specs = _page_specs(pg, (None, None, n_heads, page), 0, n_pages, pg)
    in_specs.append(pl.BlockSpec((None, n_heads, LANE), lambda b, s, pt: (b, 0, 0)))
    grid_spec = pltpu.PrefetchScalarGridSpec(
        num_scalar_prefetch=1, grid=(db, n_pages // pg), in_specs=in_specs,
        out_specs=[pl.BlockSpec((None, n_heads, pg * page), lambda b, s, pt: (b, 0, s)),
                   pl.BlockSpec((None, n_heads, LANE), lambda b, s, pt: (b, 0, 0))],
        scratch_shapes=[pltpu.VMEM((n_heads, LANE), F32)])
    return pl.pallas_call(
        functools.partial(_cum_sample_body, pg=pg), grid_spec=grid_spec,
        out_shape=[jax.ShapeDtypeStruct((db, n_heads, n_pages * page), F32),
                   jax.ShapeDtypeStruct((db, n_heads, LANE), F32)],
        compiler_params=_cparams(("arbitrary", "arbitrary")),
        name="fox_sample_cumsum")(page_table.reshape(-1), *([logf_t[None]] * pg), lfnew_t)


def _row_token(shape, n_heads, dec_seq):
    row = lax.broadcasted_iota(jnp.int32, shape, 0)
    tok = jnp.zeros(shape, jnp.int32)
    for t in range(1, dec_seq):
        tok = tok + (row >= t * n_heads).astype(jnp.int32)
    return tok


def _stage_pages(k_refs, v_refs, kb, vb, page, n_kv, want_ksum=False):
    hd = kb.shape[1] // n_kv
    ksums = []
    for i, (k_ref, v_ref) in enumerate(zip(k_refs, v_refs)):
        parts = []
        for h in range(n_kv):
            kf = k_ref[pl.ds(h, page, stride=n_kv), :]
            kb[i * page:(i + 1) * page, h * hd:(h + 1) * hd] = kf.astype(BF16)
            vb[i * page:(i + 1) * page, h * hd:(h + 1) * hd] = (
                v_ref[pl.ds(h, page, stride=n_kv), :].astype(BF16))
            if want_ksum:
                parts.append(jnp.sum(kf, axis=0, keepdims=True))
        if want_ksum:
            ksums.append(jnp.concatenate(parts, axis=1))
    return ksums


def _fox_sample_body(pt_ref, *refs, pg, page, dec_seq, n_heads, n_kv):
    k_refs, v_refs = refs[:pg], refs[pg:2 * pg]
    (q_ref, cq_ref, c_ref, cnew_ref, kn_ref, vn_ref, o_ref, kb, vb, m_sc, l_sc, acc_sc) = refs[2 * pg:]
    s = pl.program_id(1)

    @pl.when(s == 0)
    def _():
        m_sc[...] = jnp.full(m_sc.shape, -jnp.inf, F32)
        l_sc[...] = jnp.zeros(l_sc.shape, F32)
        acc_sc[...] = jnp.zeros(acc_sc.shape, F32)

    def update(logits, v):
        m_prev = m_sc[...]
        m_new = jnp.maximum(m_prev, jnp.max(logits, axis=1, keepdims=True))
        alpha = jnp.exp2(m_prev - m_new)
        pr = jnp.exp2(logits - m_new)
        l_sc[...] = alpha * l_sc[...] + jnp.sum(pr, axis=1, keepdims=True)
        acc_sc[...] = alpha * acc_sc[...] + jnp.dot(pr.astype(BF16), v, preferred_element_type=F32)
        m_sc[...] = m_new

    q = q_ref[...]
    cq = cq_ref[...]
    _stage_pages(k_refs, v_refs, kb, vb, page, n_kv)
    ck = jnp.concatenate([c_ref[...]] * dec_seq, axis=0)
    update(_dot_nt(q, kb[...]) + (cq - ck) * LOG2E, vb[...])

    @pl.when(s == pl.num_programs(1) - 1)
    def _():
        rows = q.shape[0]
        ckn = jnp.concatenate([cnew_ref[...]] * dec_seq, axis=0)
        logits = _dot_nt(q, kn_ref[...].astype(BF16)) + (cq - ckn) * LOG2E
        tok = _row_token((rows, LANE), n_heads, dec_seq)
        logits = jnp.where(lax.broadcasted_iota(jnp.int32, (rows, LANE), 1) <= tok, logits, -jnp.inf)
        update(logits, vn_ref[...].astype(BF16))
        o_ref[...] = acc_sc[...] / l_sc[...]


def _moba_sample_body(pt_ref, *refs, pg, page, dec_seq, n_heads, n_kv, past):
    k_refs, v_refs = refs[:pg], refs[pg:2 * pg]
    (q_ref, q32_ref, slope_ref, kn_ref, vn_ref, o_ref,
     kb, vb, acc_all, m_all, l_all, km) = refs[2 * pg:]
    s = pl.program_id(1)
    nblk = past // MOBA_BLOCK
    ppb = MOBA_BLOCK // page
    rows = q_ref.shape[0]
    lane = lax.broadcasted_iota(jnp.int32, (rows, LANE), 1)
    qpos = (past + _row_token((rows, 1), n_heads, dec_seq)).astype(F32)
    km_row = lax.broadcasted_iota(jnp.int32, km.shape, 0)

    @pl.when(s == 0)
    def _():
        m_all[...] = jnp.full(m_all.shape, -jnp.inf, F32)
        l_all[...] = jnp.zeros(l_all.shape, F32)
        km[...] = jnp.zeros(km.shape, F32)

    def block_partial(logits, v, blk):
        mj = jnp.max(logits, axis=1, keepdims=True)
        pr = jnp.exp2(logits - mj)
        acc_all[blk] = jnp.dot(pr.astype(BF16), v, preferred_element_type=F32)
        m_all[...] = jnp.where(lane == blk, mj, m_all[...])
        l_all[...] = jnp.where(lane == blk, jnp.sum(pr, axis=1, keepdims=True), l_all[...])

    q = q_ref[...]
    slope = slope_ref[...]
    ksums = _stage_pages(k_refs, v_refs, kb, vb, page, n_kv, want_ksum=True)
    width = pg * page
    kpos = (s * width + lax.broadcasted_iota(jnp.int32, (1, width), 1)).astype(F32)
    logits = _dot_nt(q, kb[...]) - slope * (qpos - kpos)
    for jb in range(pg // ppb):
        blk = s * (pg // ppb) + jb
        block_partial(logits[:, jb * MOBA_BLOCK:(jb + 1) * MOBA_BLOCK],
                      vb[jb * MOBA_BLOCK:(jb + 1) * MOBA_BLOCK, :], blk)
        ksum = ksums[jb * ppb]
        for i in range(1, ppb):
            ksum = ksum + ksums[jb * ppb + i]
        km[...] = jnp.where(km_row == blk, ksum * (1.0 / MOBA_BLOCK), km[...])

    @pl.when(s == pl.num_programs(1) - 1)
    def _():
        npos = (past + lax.broadcasted_iota(jnp.int32, (1, LANE), 1)).astype(F32)
        ln = _dot_nt(q, kn_ref[...].astype(BF16)) - slope * (qpos - npos)
        ln = jnp.where(npos <= qpos, ln, -jnp.inf)
        block_partial(ln, vn_ref[...].astype(BF16), nblk)
        gate = lax.dot_general(q32_ref[...], km[...], (((1,), (1,)), ((), ())),
                               preferred_element_type=F32, precision=lax.Precision.HIGHEST)
        allowed = _top3_allowed(gate, lane, jnp.full((rows, LANE), nblk, jnp.int32)) > 0.0
        m_row = jnp.max(jnp.where(allowed, m_all[...], -jnp.inf), axis=1, keepdims=True)
        w = jnp.where(allowed, jnp.exp2(m_all[...] - m_row), 0.0)
        denom = jnp.sum(w * l_all[...], axis=1, keepdims=True)
        out = jnp.zeros(o_ref.shape, F32)
        for jb in range(nblk + 1):
            out = out + w[:, jb:jb + 1] * acc_all[jb]
        o_ref[...] = out / denom


def _sample_attention(page_table, cache_k4, cache_v4, layer, qbd, kn_pad, vn_pad, extra, *, moba,
                      dec_seq, n_heads):
    db, n_pages = page_table.shape
    kd = qbd.shape[2]
    hd = cache_k4.shape[3]
    n_kv = kd // hd
    page = cache_k4.shape[2] // n_kv
    rows = qbd.shape[1]
    past = n_pages * page
    pg = math.gcd(PAGES_PER_STEP, n_pages)
    assert page == LANE and MOBA_BLOCK % page == 0 and pg % (MOBA_BLOCK // page) == 0
    assert past % MOBA_BLOCK == 0 and past // MOBA_BLOCK < LANE
    page_specs = (_page_specs(pg, (None, None, page * n_kv, hd), layer, n_pages, pg)
                  + _page_specs(pg, (None, None, page * n_kv, hd), layer, n_pages, pg))
    per_b = lambda a: pl.BlockSpec((None,) + a.shape[1:], lambda b, s, pt: (b,) + (0,) * (a.ndim - 1))
    common = [pltpu.VMEM((pg * page, kd), BF16), pltpu.VMEM((pg * page, kd), BF16)]
    if moba:
        q32, slope = extra
        ins = [qbd, q32, slope, kn_pad, vn_pad]
        in_specs = page_specs + [per_b(qbd), per_b(q32),
                                 pl.BlockSpec(slope.shape, lambda b, s, pt: (0, 0)),
                                 per_b(kn_pad), per_b(vn_pad)]
        scratch = common + [pltpu.VMEM((past // MOBA_BLOCK + 1, rows, kd), F32),
                            pltpu.VMEM((rows, LANE), F32), pltpu.VMEM((rows, LANE), F32),
                            pltpu.VMEM((LANE, kd), F32)]
        body = functools.partial(_moba_sample_body, pg=pg, page=page, dec_seq=dec_seq,
                                 n_heads=n_heads, n_kv=n_kv, past=past)
    else:
        cq, c_t, cnew_t = extra
        ins = [qbd, cq, c_t, cnew_t, kn_pad, vn_pad]
        in_specs = page_specs + [per_b(qbd), per_b(cq),
                                 pl.BlockSpec((None, n_heads, pg * page), lambda b, s, pt: (b, 0, s)),
                                 per_b(cnew_t), per_b(kn_pad), per_b(vn_pad)]
        scratch = common + [pltpu.VMEM((rows, 1), F32), pltpu.VMEM((rows, 1), F32),
                            pltpu.VMEM((rows, kd), F32)]
        body = functools.partial(_fox_sample_body, pg=pg, page=page, dec_seq=dec_seq, n_heads=n_heads,
                                 n_kv=n_kv)
    grid_spec = pltpu.PrefetchScalarGridSpec(
        num_scalar_prefetch=1, grid=(db, n_pages // pg), in_specs=in_specs,
        out_specs=pl.BlockSpec((None, rows, kd), lambda b, s, pt: (b, 0, 0)), scratch_shapes=scratch)
    return pl.pallas_call(
        body, grid_spec=grid_spec, out_shape=jax.ShapeDtypeStruct((db, rows, kd), F32),
        compiler_params=_cparams(("arbitrary", "arbitrary")),
        name="moba_sample" if moba else "fox_sample")(
            page_table.reshape(-1), *([cache_k4] * pg), *([cache_v4] * pg), *ins)


def _wo_body(x_ref, mix_ref, w_ref, o_ref):
    o_ref[...] = x_ref[...] + jnp.dot(mix_ref[...], w_ref[...], preferred_element_type=F32)


def _wo_residual(x, mix_bf, wo_bf, layer, *, tm):
    rows, d = x.shape
    tm = _row_tile(rows, tm)
    return pl.pallas_call(
        _wo_body, grid=(rows // tm,),
        in_specs=[pl.BlockSpec((tm, d), lambda i: (i, 0)),
                  pl.BlockSpec((tm, mix_bf.shape[1]), lambda i: (i, 0)),
                  pl.BlockSpec((None,) + wo_bf.shape[1:], lambda i: (layer, 0, 0))],
        out_specs=pl.BlockSpec((tm, d), lambda i: (i, 0)),
        out_shape=jax.ShapeDtypeStruct((rows, d), F32),
        compiler_params=_cparams(("arbitrary",)), name="wo_residual")(x, mix_bf, wo_bf)


def _mlp_body(h_ref, g_ref, wu_ref, wd_ref, o_ref, hn_sc):
    f = pl.program_id(1)

    @pl.when(f == 0)
    def _():
        h = h_ref[...]
        ms = jnp.mean(h * h, axis=-1, keepdims=True)
        hn_sc[...] = (h * lax.rsqrt(ms + RMS_EPS) * g_ref[...]).astype(BF16)
        o_ref[...] = h

    u = jnp.maximum(jnp.dot(hn_sc[...], wu_ref[...], preferred_element_type=F32), 0.0)
    o_ref[...] += jnp.dot((u * u).astype(BF16), wd_ref[...], preferred_element_type=F32)


def _mlp_residual(h, g_mlp, w_up, w_down, layer, *, tm, tf):
    rows, d = h.shape
    dff = w_up.shape[2]
    tm = _row_tile(rows, tm)
    tf = min(tf, dff)
    assert dff % tf == 0
    return pl.pallas_call(
        _mlp_body, grid=(rows // tm, dff // tf),
        in_specs=[pl.BlockSpec((tm, d), lambda i, f: (i, 0)),
                  pl.BlockSpec(g_mlp.shape, lambda i, f: (0, 0)),
                  pl.BlockSpec((None, d, tf), lambda i, f: (layer, 0, f)),
                  pl.BlockSpec((None, tf, d), lambda i, f: (layer, f, 0))],
        out_specs=pl.BlockSpec((tm, d), lambda i, f: (i, 0)),
        out_shape=jax.ShapeDtypeStruct((rows, d), F32),
        scratch_shapes=[pltpu.VMEM((tm, d), BF16)],
        compiler_params=_cparams(("arbitrary", "arbitrary")),
        name="mlp_residual")(h, g_mlp, w_up, w_down)


def kernel(x_prompt, x_sample, cache_k, cache_v, cache_logf, page_table, attn_norm, w_qkv, q_norm, k_norm,
           w_fgate, b_fgate, w_o, mlp_norm, w_up, w_down):
    batch, seq, d = x_prompt.shape
    db, dec_seq, _ = x_sample.shape
    depth, n_pool, page, n_kv, hd = cache_k.shape
    n_heads = w_fgate.shape[2]
    group = n_heads // n_kv
    qd, kd = n_heads * hd, n_kv * hd
    n_pages = page_table.shape[1]
    past = n_pages * page
    rows_s = dec_seq * n_heads

    hp = x_prompt.reshape(batch * seq, d)
    hs = x_sample.reshape(db * dec_seq, d)
    cache_k4 = cache_k.reshape(depth, n_pool, page * n_kv, hd)
    cache_v4 = cache_v.reshape(depth, n_pool, page * n_kv, hd)
    row_kv = (jnp.arange(rows_s) % n_heads) // group
    row_mask = (row_kv[:, None] == jnp.arange(n_kv)[None, :]).astype(F32)
    slopes = jnp.exp2(-8.0 * jnp.arange(1, n_heads + 1, dtype=F32) / n_heads)

    def block_diag(q2d, dtype):
        q = q2d.astype(F32).reshape(db, rows_s, 1, hd) * row_mask[None, :, :, None]
        return q.reshape(db, rows_s, kd).astype(dtype)

    def pad_page(a2d):
        a = a2d.reshape(db, dec_seq, kd)
        return jnp.pad(a, ((0, 0), (0, LANE - dec_seq), (0, 0)))

    def own_lanes(o):
        o6 = o.reshape(db, dec_seq, n_kv, group, n_kv, hd)
        return jnp.stack([o6[:, :, k, :, k, :] for k in range(n_kv)], axis=2).reshape(db * dec_seq, qd)

    outs = {k: [] for k in ("kp", "vp", "fp", "ks", "vs", "fs")}
    w_qkv_bf, wo_bf = w_qkv.astype(BF16), w_o.astype(BF16)
    w_up_bf, w_down_bf = w_up.astype(BF16), w_down.astype(BF16)
    for i in range(depth):
        fox = i % 2 == 0
        j = i // 2
        g_attn, qg, kg = attn_norm[i][None, :], q_norm[i][None, :], k_norm[i][None, :]
        if fox:
            wf_bf = jnp.pad(w_fgate[j], ((0, 0), (0, LANE - n_heads))).astype(BF16)
            bf_pad = jnp.pad(b_fgate[j], (0, LANE - n_heads))[None, :]
        else:
            wf_bf = bf_pad = None
        proj = functools.partial(_qkv_proj, g_attn=g_attn, w_bf=w_qkv_bf, q_gain=qg, k_gain=kg, wf_bf=wf_bf,
                                 bf_pad=bf_pad, layer=i, n_heads=n_heads, n_kv=n_kv, hd=hd, q_f32=not fox)
        res_p = proj(hp, attn_copies=True, tm=512)
        res_s = proj(hs, attn_copies=False, tm=128)
        qp, kp32, vp32, kpb, vpt = res_p[:5]
        qs, ks32, vs32 = res_s[:3]
        outs["kp"].append(kp32.reshape(batch, seq, n_kv, hd))
        outs["vp"].append(vp32.reshape(batch, seq, n_kv, hd))
        outs["ks"].append(ks32.reshape(db, dec_seq, n_kv, hd))
        outs["vs"].append(vs32.reshape(db, dec_seq, n_kv, hd))
        kn_pad, vn_pad = pad_page(ks32), pad_page(vs32)
        if fox:
            lfp, lfp16, lfs16 = res_p[5], res_p[6], res_s[4]
            outs["fp"].append(lfp16.reshape(batch, seq, n_heads))
            outs["fs"].append(lfs16.reshape(db, dec_seq, n_heads))
            qx, kx = _fox_aug(lfp, batch=batch, seq=seq, n_heads=n_heads, n_kv=n_kv)
            mix_p = _flash_prompt(qp, qx, kpb, kx, vpt, None, batch=batch, seq=seq, n_heads=n_heads,
                                  n_kv=n_kv, hd=hd, moba=False)
            logf_t = jnp.swapaxes(cache_logf[j], 1, 2)
            lfnew_t = jnp.pad(jnp.swapaxes(lfs16.reshape(db, dec_seq, n_heads), 1, 2),
                              ((0, 0), (0, 0), (0, LANE - dec_seq)))
            c_t, cnew_t = _cum_sample(page_table, logf_t, lfnew_t, n_heads=n_heads)
            cq = jnp.swapaxes(cnew_t[:, :, :dec_seq], 1, 2).reshape(db, rows_s, 1)
            o_s = _sample_attention(page_table, cache_k4, cache_v4, i, block_diag(qs, BF16), kn_pad, vn_pad,
                                    (cq, c_t, cnew_t), moba=False, dec_seq=dec_seq, n_heads=n_heads)
        else:
            kmean = _kmean(kp32, batch=batch, seq=seq, n_kv=n_kv)
            kmt = jnp.swapaxes(kmean.reshape(batch, LANE, n_kv, hd), 1, 3)
            kmt = jnp.swapaxes(kmt, 1, 2)
            qx, kx = _moba_static_cols(seq, n_heads)
            mix_p = _flash_prompt(qp, qx, kpb, kx, vpt, kmt, batch=batch, seq=seq, n_heads=n_heads,
                                  n_kv=n_kv, hd=hd, moba=True)
            slope_col = (jnp.tile(slopes, dec_seq) * LOG2E)[:, None]
            q_scaled = qs * (hd ** -0.5 * LOG2E)
            o_s = _sample_attention(page_table, cache_k4, cache_v4, i, block_diag(q_scaled, BF16), kn_pad,
                                    vn_pad, (block_diag(qs, F32), slope_col), moba=True, dec_seq=dec_seq,
                                    n_heads=n_heads)
        mix_s = own_lanes(o_s).astype(BF16)
        g_mlp = mlp_norm[i][None, :]
        hp = _wo_residual(hp, mix_p, wo_bf, i, tm=512)
        hs = _wo_residual(hs, mix_s, wo_bf, i, tm=128)
        hp = _mlp_residual(hp, g_mlp, w_up_bf, w_down_bf, i, tm=1024, tf=512)
        hs = _mlp_residual(hs, g_mlp, w_up_bf, w_down_bf, i, tm=128, tf=512)
    return (hp.reshape(batch, seq, d), hs.reshape(db, dec_seq, d),
            jnp.stack(outs["kp"]), jnp.stack(outs["vp"]), jnp.stack(outs["fp"]),
            jnp.stack(outs["ks"]), jnp.stack(outs["vs"]), jnp.stack(outs["fs"]))
```

```python
import functools
import math

import numpy as np
import jax
import jax.numpy as jnp
from jax import lax
from jax.experimental import pallas as pl
from jax.experimental.pallas import tpu as pltpu

F32 = jnp.float32
BF16 = jnp.bfloat16
LOG2E = 1.4426950408889634
RMS_EPS = 1e-6
MOBA_BLOCK = 256
MOBA_TOPK = 3
LANE = 128
ATTN_TILE = 512
FLASH_CHUNK = 512
DENOM_ROWS = 16
FIXED_MAX_HEADROOM = 64.0
PAGES_PER_STEP = 32
NEG_BIG = -(2.0 ** 100)
VMEM_LIMIT = 56 * 1024 * 1024
COL_BLK = 0
COL_QPOS = 64
COL_KPOS = 67


def _cparams(sem):
    return pltpu.CompilerParams(dimension_semantics=sem, vmem_limit_bytes=VMEM_LIMIT)


def _row_tile(rows, pref):
    t = min(rows, pref)
    while rows % t:
        t -= 8
    return t


def _split3(x):
    hi = x.astype(BF16).astype(F32)
    r1 = x - hi
    mid = r1.astype(BF16).astype(F32)
    lo = (r1 - mid).astype(BF16).astype(F32)
    return hi, mid, lo


def _dot_nt(a, b):
    return lax.dot_general(a, b, (((1,), (1,)), ((), ())), preferred_element_type=F32)


def _qkv_body(*refs, n_heads, n_kv, hd, q_scale, with_gate, q_f32, attn_copies):
    n_in = 7 if with_gate else 5
    x_ref, g_ref, w_ref, qg_ref, kg_ref = refs[:5]
    outs = list(refs[n_in:])
    q_ref, k32_ref, v32_ref = outs[:3]
    del outs[:3]
    if attn_copies:
        kb_ref, vt_ref = outs[:2]
        del outs[:2]
    qd, kd = n_heads * hd, n_kv * hd
    x = x_ref[...]
    ms = jnp.mean(x * x, axis=-1, keepdims=True)
    xn = (x * lax.rsqrt(ms + RMS_EPS) * g_ref[...]).astype(BF16)

    def head_norm(a, gain):
        m = jnp.mean(a * a, axis=-1, keepdims=True)
        return a * lax.rsqrt(m + RMS_EPS) * gain

    chunk = 4 * hd
    for c in range(qd // chunk):
        acc = jnp.dot(xn, w_ref[:, c * chunk:(c + 1) * chunk], preferred_element_type=F32)
        for j in range(4):
            qn = head_norm(acc[:, j * hd:(j + 1) * hd], qg_ref[...])
            lo = c * chunk + j * hd
            if q_f32:
                q_ref[:, lo:lo + hd] = qn
            else:
                q_ref[:, lo:lo + hd] = (qn * q_scale).astype(BF16)
    tm = x.shape[0]
    acc = jnp.dot(xn, w_ref[:, qd:qd + kd], preferred_element_type=F32)
    for j in range(n_kv):
        kn = head_norm(acc[:, j * hd:(j + 1) * hd], kg_ref[...])
        k32_ref[pl.ds(j, tm, stride=n_kv), :] = kn
        if attn_copies:
            kb_ref[:, j * hd:(j + 1) * hd] = kn.astype(BF16)
    acc = jnp.dot(xn, w_ref[:, qd + kd:qd + 2 * kd], preferred_element_type=F32)
    for j in range(n_kv):
        v32_ref[pl.ds(j, tm, stride=n_kv), :] = acc[:, j * hd:(j + 1) * hd]
    if attn_copies:
        vt_ref[...] = acc.T.astype(BF16)
    if with_gate:
        wf_ref, bf_ref = refs[5:7]
        z = jnp.dot(xn, wf_ref[...], preferred_element_type=F32) + bf_ref[...]
        lf = jnp.minimum(z, 0.0) - jnp.log1p(jnp.exp(-jnp.abs(z)))
        outs[0][...] = lf
        outs[1][...] = lf[:, :n_heads]


def _qkv_proj(x, g_attn, w_bf, q_gain, k_gain, wf_bf, bf_pad, *, layer, n_heads, n_kv, hd, q_f32,
              attn_copies, tm):
    rows, d = x.shape
    qd, kd = n_heads * hd, n_kv * hd
    with_gate = wf_bf is not None
    tm = _row_tile(rows, tm)
    row_spec = lambda w: pl.BlockSpec((tm, w), lambda i: (i, 0))
    full = lambda a: pl.BlockSpec(a.shape, lambda i: (0,) * a.ndim)
    ins = [x, g_attn, w_bf, q_gain, k_gain]
    in_specs = [row_spec(d), full(g_attn), pl.BlockSpec((None,) + w_bf.shape[1:], lambda i: (layer, 0, 0)),
                full(q_gain), full(k_gain)]
    kv_spec = pl.BlockSpec((tm * n_kv, hd), lambda i: (i, 0))
    out_shape = [jax.ShapeDtypeStruct((rows, qd), F32 if q_f32 else BF16),
                 jax.ShapeDtypeStruct((rows * n_kv, hd), F32), jax.ShapeDtypeStruct((rows * n_kv, hd), F32)]
    out_specs = [row_spec(qd), kv_spec, kv_spec]
    if attn_copies:
        out_shape += [jax.ShapeDtypeStruct((rows, kd), BF16), jax.ShapeDtypeStruct((kd, rows), BF16)]
        out_specs += [row_spec(kd), pl.BlockSpec((kd, tm), lambda i: (0, i))]
    if with_gate:
        ins += [wf_bf, bf_pad]
        in_specs += [full(wf_bf), full(bf_pad)]
        out_shape += [jax.ShapeDtypeStruct((rows, LANE), F32), jax.ShapeDtypeStruct((rows, n_heads), F32)]
        out_specs += [row_spec(LANE), row_spec(n_heads)]
    body = functools.partial(_qkv_body, n_heads=n_heads, n_kv=n_kv, hd=hd,
                             q_scale=hd ** -0.5 * LOG2E, with_gate=with_gate, q_f32=q_f32,
                             attn_copies=attn_copies)
    return pl.pallas_call(body, grid=(rows // tm,), in_specs=in_specs, out_specs=out_specs,
                          out_shape=out_shape, compiler_params=_cparams(("arbitrary",)),
                          name="qkv_proj")(*ins)


def _fox_aug_body(lf_ref, qx_ref, kx_ref, carry_ref, *, n_heads, n_kv):
    t = pl.program_id(1)

    @pl.when(t == 0)
    def _():
        carry_ref[...] = jnp.zeros_like(carry_ref)

    lf = lf_ref[...]
    tc = lf.shape[0]
    tri = (lax.broadcasted_iota(jnp.int32, (tc, tc), 1)
           <= lax.broadcasted_iota(jnp.int32, (tc, tc), 0)).astype(F32)
    c = jnp.dot(tri, lf, preferred_element_type=F32, precision=lax.Precision.HIGHEST) + carry_ref[...]
    carry_ref[...] = c[tc - 1:tc, :]
    c2 = c * LOG2E
    lane = lax.broadcasted_iota(jnp.int32, (tc, LANE), 1)
    group = n_heads // n_kv
    ones3 = jnp.where(lane < 3, 1.0, 0.0)
    for kvh in range(n_kv):
        kaug = ones3
        for g in range(group):
            h = kvh * group + g
            hi, mid, lo = _split3(jnp.broadcast_to(c2[:, h:h + 1], (tc, LANE)))
            base = 3 + 3 * g
            own = jnp.where((lane >= base) & (lane < base + 3), 1.0, 0.0)
            qaug = jnp.where(lane == 0, hi, jnp.where(lane == 1, mid, jnp.where(lane == 2, lo, own)))
            qx_ref[:, h * LANE:(h + 1) * LANE] = qaug.astype(BF16)
            kaug = jnp.where(lane == base, -hi,
                             jnp.where(lane == base + 1, -mid, jnp.where(lane == base + 2, -lo, kaug)))
        kx_ref[:, kvh * LANE:(kvh + 1) * LANE] = kaug.astype(BF16)


def _fox_aug(lf128, *, batch, seq, n_heads, n_kv):
    tc = _row_tile(seq, 512)
    nt = seq // tc
    body = functools.partial(_fox_aug_body, n_heads=n_heads, n_kv=n_kv)
    return pl.pallas_call(
        body, grid=(batch, nt),
        in_specs=[pl.BlockSpec((tc, LANE), lambda b, t: (b * nt + t, 0))],
        out_specs=[pl.BlockSpec((tc, n_heads * LANE), lambda b, t: (b * nt + t, 0)),
                   pl.BlockSpec((tc, n_kv * LANE), lambda b, t: (b * nt + t, 0))],
        out_shape=[jax.ShapeDtypeStruct((batch * seq, n_heads * LANE), BF16),
                   jax.ShapeDtypeStruct((batch * seq, n_kv * LANE), BF16)],
        scratch_shapes=[pltpu.VMEM((1, LANE), F32)],
        compiler_params=_cparams(("arbitrary", "arbitrary")), name="fox_aug")(lf128)


def _kmean_body(k_ref, o_ref, *, nblk, n_kv):
    hd = k_ref.shape[1]
    for j in range(nblk):
        for h in range(n_kv):
            blk = k_ref[pl.ds(j * MOBA_BLOCK * n_kv + h, MOBA_BLOCK, stride=n_kv), :]
            o_ref[j:j + 1, h * hd:(h + 1) * hd] = jnp.sum(blk, axis=0, keepdims=True) * (1.0 / MOBA_BLOCK)
    if o_ref.shape[0] > nblk:
        o_ref[nblk:, :] = jnp.zeros((o_ref.shape[0] - nblk, o_ref.shape[1]), F32)


def _kmean(k32, *, batch, seq, n_kv):
    hd = k32.shape[1]
    nblk = seq // MOBA_BLOCK
    assert nblk <= COL_QPOS
    return pl.pallas_call(
        functools.partial(_kmean_body, nblk=nblk, n_kv=n_kv), grid=(batch,),
        in_specs=[pl.BlockSpec((seq * n_kv, hd), lambda b: (b, 0))],
        out_specs=pl.BlockSpec((None, LANE, n_kv * hd), lambda b: (b, 0, 0)),
        out_shape=jax.ShapeDtypeStruct((batch, LANE, n_kv * hd), F32),
        compiler_params=_cparams(("arbitrary",)), name="moba_kmean")(k32)


def _top3_allowed(gate, lane, own):
    lane_f = lane.astype(F32)
    gm = jnp.where(lane < own, gate, -jnp.inf)
    sel = jnp.zeros(gate.shape, F32)
    for _ in range(MOBA_TOPK):
        mx = jnp.max(gm, axis=1, keepdims=True)
        idx = jnp.min(jnp.where(gm == mx, lane_f, float(LANE)), axis=1, keepdims=True)
        pick = lane_f == idx
        sel = jnp.where(pick, 1.0, sel)
        gm = jnp.where(pick, -jnp.inf, gm)
    return jnp.where(lane < own, sel, jnp.where(lane == own, 1.0, 0.0))


def _fixed_max_is_safe(q_gain, k_gain, hd):
    b = math.sqrt(hd) * LOG2E * jnp.max(jnp.abs(q_gain)) * jnp.max(jnp.abs(k_gain))
    return (2.0 * 1.02 * b + 2.0 < FIXED_MAX_HEADROOM).astype(jnp.int32).reshape(1)


def _flash_body(qi_ref, kj_ref, last_ref, safe_ref, *refs, group, hd, moba, tile, nblk):
    if moba:
        q_ref, qx_ref, k_ref, kx_ref, vt_ref, km_ref, o_ref, qaug, m_sc, acc_sc, qlo = refs
    else:
        q_ref, qx_ref, k_ref, kx_ref, vt_ref, o_ref, qaug, m_sc, acc_sc = refs
    p = pl.program_id(2)
    i, j = qi_ref[p], kj_ref[p]

    def moba_selection_bias():
        nbr = -(-nblk // 16) * 16
        km = km_ref[:nbr, :]
        km_hi = km.astype(BF16)
        km_lo = (km - km_hi.astype(F32)).astype(BF16)
        g1 = _dot_nt(jnp.concatenate([km_hi, km_lo], axis=0), qaug[:, :hd])
        gate = g1[:nbr, :] + g1[nbr:, :] + _dot_nt(km_hi, qlo[...])
        blk = lax.broadcasted_iota(jnp.int32, gate.shape, 0)
        pos = i * tile + (lax.broadcasted_iota(jnp.int32, gate.shape, 1) & (tile - 1))
        own = lax.shift_right_logical(pos, int(math.log2(MOBA_BLOCK)))
        blk_f = blk.astype(F32)
        gm = jnp.where(blk < own, gate, -jnp.inf)
        sel = jnp.zeros(gate.shape, F32)
        for _ in range(MOBA_TOPK):
            mx = jnp.max(gm, axis=0, keepdims=True)
            idx = jnp.min(jnp.where(gm == mx, blk_f, float(LANE)), axis=0, keepdims=True)
            pick = blk_f == idx
            sel = jnp.where(pick, 1.0, sel)
            gm = jnp.where(pick, -jnp.inf, gm)
        allowed = jnp.where(blk < own, sel, jnp.where(blk == own, 1.0, 0.0))
        bias = jnp.where(allowed > 0.0, 0.0, NEG_BIG)
        return jnp.concatenate([bias, jnp.zeros((LANE - nbr, gate.shape[1]), F32)], axis=0)

    @pl.when(j == i)
    def _():
        for g in range(group):
            rows = slice(g * tile, (g + 1) * tile)
            if moba:
                qs = q_ref[:, g * hd:(g + 1) * hd] * (hd ** -0.5 * LOG2E)
                hi = qs.astype(BF16)
                qaug[rows, :hd] = hi
                qlo[rows, :] = (qs - hi.astype(F32)).astype(BF16)
            else:
                qaug[rows, :hd] = q_ref[:, g * hd:(g + 1) * hd]
                qaug[rows, hd:] = qx_ref[:, g * LANE:(g + 1) * LANE]
        if moba:
            bias_t = moba_selection_bias()
            lane = lax.broadcasted_iota(jnp.int32, (tile, LANE), 1)
            for g in range(group):
                aug = jnp.where(lane < COL_QPOS, bias_t[:, g * tile:(g + 1) * tile].T,
                                qx_ref[:, g * LANE:(g + 1) * LANE].astype(F32))
                qaug[g * tile:(g + 1) * tile, hd:] = aug.astype(BF16)
        m_sc[...] = jnp.full(m_sc.shape, -jnp.inf, F32)
        acc_sc[...] = jnp.zeros(acc_sc.shape, F32)

    def step(diagonal, fixed_max=False):
        kaug = jnp.concatenate([k_ref[...], kx_ref[...]], axis=1)
        vt = jnp.concatenate([vt_ref[...], jnp.ones((DENOM_ROWS, tile), BF16)], axis=0)
        n_chunks = group * tile // FLASH_CHUNK
        scores = [_dot_nt(kaug, qaug[0:FLASH_CHUNK, :])]
        for c in range(n_chunks):
            cols = slice(c * FLASH_CHUNK, (c + 1) * FLASH_CHUNK)
            if c + 1 < n_chunks:
                scores.append(_dot_nt(kaug, qaug[(c + 1) * FLASH_CHUNK:(c + 2) * FLASH_CHUNK, :]))
            st = scores[c]
            if fixed_max:
                pt = jnp.exp2(st - m_sc[:, cols])
                acc_sc[:, cols] += jnp.dot(vt, pt.astype(BF16), preferred_element_type=F32)
                continue
            if diagonal:
                key = lax.broadcasted_iota(jnp.int32, st.shape, 0)
                query = (c * FLASH_CHUNK) % tile + lax.broadcasted_iota(jnp.int32, st.shape, 1)
                st = jnp.where(key <= query, st, -jnp.inf)
            m_prev = m_sc[:, cols]
            m_new = jnp.maximum(m_prev, jnp.max(st, axis=0, keepdims=True))
            alpha = jnp.exp2(m_prev - m_new)
            pt = jnp.exp2(st - m_new)
            acc_sc[:, cols] = alpha * acc_sc[:, cols] + jnp.dot(vt, pt.astype(BF16),
                                                                preferred_element_type=F32)
            m_sc[:, cols] = m_new

    fixed = safe_ref[0] == 1
    pl.when(j == i)(lambda: step(True))
    pl.when(jnp.logical_and(j != i, fixed))(lambda: step(False, fixed_max=True))
    pl.when(jnp.logical_and(j != i, jnp.logical_not(fixed)))(lambda: step(False))

    @pl.when(last_ref[p] == 1)
    def _():
        out_t = acc_sc[:hd, :] / acc_sc[hd:hd + 1, :]
        for g in range(group):
            o_ref[:, g * hd:(g + 1) * hd] = out_t[:, g * tile:(g + 1) * tile].T.astype(o_ref.dtype)


def _flash_prompt(q, qx, kb, kx, vt, kmean, safe, *, batch, seq, n_heads, n_kv, hd, moba):
    tile = ATTN_TILE
    assert seq % tile == 0 and tile % MOBA_BLOCK == 0 and tile & (tile - 1) == 0
    nt = seq // tile
    group = n_heads // n_kv
    qi = np.concatenate([np.full(i + 1, i) for i in range(nt)]).astype(np.int32)
    kj = np.concatenate([np.array([i] + list(range(i))) for i in range(nt)]).astype(np.int32)
    last = np.concatenate([np.array([0] * i + [1]) for i in range(nt)]).astype(np.int32)
    gw = group * hd
    q_map = lambda b, h, p, qi, kj, *_: (b * nt + qi[p], h)
    k_map = lambda b, h, p, qi, kj, *_: (b * nt + kj[p], h)
    vt_spec = pl.BlockSpec((hd, tile), lambda b, h, p, qi, kj, *_: (h, b * nt + kj[p]))
    in_specs = [pl.BlockSpec((tile, gw), q_map)]
    if moba:
        in_specs += [pl.BlockSpec((tile, group * LANE), lambda b, h, p, qi, kj, *_: (qi[p], h)),
                     pl.BlockSpec((tile, hd), k_map),
                     pl.BlockSpec((tile, LANE), lambda b, h, p, qi, kj, *_: (kj[p], 0)),
                     vt_spec,
                     pl.BlockSpec((None, LANE, hd), lambda b, h, p, qi, kj, *_: (b, 0, h))]
        ins = [q, qx, kb, kx, vt, kmean]
    else:
        in_specs += [pl.BlockSpec((tile, group * LANE), q_map),
                     pl.BlockSpec((tile, hd), k_map),
                     pl.BlockSpec((tile, LANE), k_map),
                     vt_spec]
        ins = [q, qx, kb, kx, vt]
    body = functools.partial(_flash_body, group=group, hd=hd, moba=moba, tile=tile,
                             nblk=seq // MOBA_BLOCK)
    scratch = [pltpu.VMEM((group * tile, hd + LANE), BF16), pltpu.VMEM((1, group * tile), F32),
               pltpu.VMEM((hd + DENOM_ROWS, group * tile), F32)]
    if moba:
        scratch.append(pltpu.VMEM((group * tile, hd), BF16))
    grid_spec = pltpu.PrefetchScalarGridSpec(
        num_scalar_prefetch=4, grid=(batch, n_kv, len(qi)), in_specs=in_specs,
        out_specs=pl.BlockSpec((tile, gw), q_map), scratch_shapes=scratch)
    return pl.pallas_call(
        body, grid_spec=grid_spec, out_shape=jax.ShapeDtypeStruct((batch * seq, n_heads * hd), BF16),
        compiler_params=_cparams(("arbitrary", "arbitrary", "arbitrary")),
        name="moba_prompt" if moba else "fox_prompt")(jnp.asarray(qi), jnp.asarray(kj),
                                                      jnp.asarray(last), safe, *ins)


def _moba_static_cols(seq, n_heads):
    pos = jnp.arange(seq, dtype=jnp.int32)
    slopes = jnp.exp2(-8.0 * jnp.arange(1, n_heads + 1, dtype=F32) / n_heads) * LOG2E
    qparts = _split3(-(slopes[None, :] * pos.astype(F32)[:, None]))
    sparts = _split3(slopes)
    qx = jnp.zeros((seq, n_heads, LANE), F32)
    for c in range(3):
        qx = qx.at[:, :, COL_QPOS + c].set(qparts[c])
        qx = qx.at[:, :, COL_KPOS + c].set(jnp.broadcast_to(sparts[c][None, :], (seq, n_heads)))
        qx = qx.at[:, :, COL_KPOS + 3 + c].set(jnp.broadcast_to(sparts[c][None, :], (seq, n_heads)))
    blk = pos // MOBA_BLOCK
    lane = jnp.arange(LANE, dtype=jnp.int32)[None, :]
    kx = jnp.where(lane == blk[:, None], 1.0, 0.0)
    kx = jnp.where((lane >= COL_QPOS) & (lane < COL_QPOS + 3), 1.0, kx)
    kx = jnp.where((lane >= COL_KPOS) & (lane < COL_KPOS + 3), (blk * MOBA_BLOCK).astype(F32)[:, None], kx)
    kx = jnp.where((lane >= COL_KPOS + 3) & (lane < COL_KPOS + 6), (pos % MOBA_BLOCK).astype(F32)[:, None], kx)
    return qx.reshape(seq, n_heads * LANE).astype(BF16), kx.astype(BF16)


def _page_specs(n, block, layer_arg, n_pages, pages_per_step):
    def spec(i):
        def imap(b, s, pt):
            return (layer_arg, pt[b * n_pages + s * pages_per_step + i]) + (0,) * (len(block) - 2)
        return pl.BlockSpec(block, imap)
    return [spec(i) for i in range(n)]


def _cum_sample_body(pt_ref, *refs, pg):
    lf_refs, new_ref, c_ref, cnew_ref, carry_ref = refs[:pg], refs[pg], refs[pg + 1], refs[pg + 2], refs[pg + 3]
    s = pl.program_id(1)

    @pl.when(s == 0)
    def _():
        carry_ref[...] = jnp.zeros_like(carry_ref)

    triu = (lax.broadcasted_iota(jnp.int32, (LANE, LANE), 0)
            <= lax.broadcasted_iota(jnp.int32, (LANE, LANE), 1)).astype(F32)
    carry = carry_ref[...]
    nh = carry.shape[0]
    local = jnp.dot(jnp.concatenate([r[...] for r in lf_refs], axis=0), triu,
                    preferred_element_type=F32, precision=lax.Precision.HIGHEST)
    for i in range(pg):
        loc = local[i * nh:(i + 1) * nh, :]
        c_ref[:, i * LANE:(i + 1) * LANE] = loc + carry
        carry = carry + jnp.broadcast_to(loc[:, LANE - 1:LANE], loc.shape)
    carry_ref[...] = carry

    @pl.when(s == pl.num_programs(1) - 1)
    def _():
        cnew_ref[...] = jnp.dot(new_ref[...], triu, preferred_element_type=F32,
                                precision=lax.Precision.HIGHEST) + carry


def _cum_sample(page_table, logf_t, lfnew_t, *, n_heads):
    db, n_pages = page_table.shape
    page = logf_t.shape[2]
    assert page == LANE
    pg = math.gcd(PAGES_PER_STEP, n_pages)
    in_specs = _page_specs(pg, (None, None, n_heads, page), 0, n_pages, pg)
    in_specs.append(pl.BlockSpec((None, n_heads, LANE), lambda b, s, pt: (b, 0, 0)))
    grid_spec = pltpu.PrefetchScalarGridSpec(
        num_scalar_prefetch=1, grid=(db, n_pages // pg), in_specs=in_specs,
        out_specs=[pl.BlockSpec((None, n_heads, pg * page), lambda b, s, pt: (b, 0, s)),
                   pl.BlockSpec((None, n_heads, LANE), lambda b, s, pt: (b, 0, 0))],
        scratch_shapes=[pltpu.VMEM((n_heads, LANE), F32)])
    return pl.pallas_call(
        functools.partial(_cum_sample_body, pg=pg), grid_spec=grid_spec,
        out_shape=[jax.ShapeDtypeStruct((db, n_heads, n_pages * page), F32),
                   jax.ShapeDtypeStruct((db, n_heads, LANE), F32)],
        compiler_params=_cparams(("arbitrary", "arbitrary")),
        name="fox_sample_cumsum")(page_table.reshape(-1), *([logf_t[None]] * pg), lfnew_t)


def _row_token(shape, n_heads, dec_seq):
    row = lax.broadcasted_iota(jnp.int32, shape, 0)
    tok = jnp.zeros(shape, jnp.int32)
    for t in range(1, dec_seq):
        tok = tok + (row >= t * n_heads).astype(jnp.int32)
    return tok


def _stage_pages(k_refs, v_refs, kb, vb, page, n_kv, want_ksum=False):
    hd = kb.shape[1] // n_kv
    ksums = []
    for i, (k_ref, v_ref) in enumerate(zip(k_refs, v_refs)):
        parts = []
        for h in range(n_kv):
            kf = k_ref[pl.ds(h, page, stride=n_kv), :]
            kb[i * page:(i + 1) * page, h * hd:(h + 1) * hd] = kf.astype(BF16)
            vb[i * page:(i + 1) * page, h * hd:(h + 1) * hd] = (
                v_ref[pl.ds(h, page, stride=n_kv), :].astype(BF16))
            if want_ksum:
                parts.append(jnp.sum(kf, axis=0, keepdims=True))
        if want_ksum:
            ksums.append(jnp.concatenate(parts, axis=1))
    return ksums


def _fox_sample_body(pt_ref, *refs, pg, page, dec_seq, n_heads, n_kv):
    k_refs, v_refs = refs[:pg], refs[pg:2 * pg]
    (q_ref, cq_ref, c_ref, cnew_ref, kn_ref, vn_ref, o_ref, kb, vb, m_sc, l_sc, acc_sc) = refs[2 * pg:]
    s = pl.program_id(1)

    @pl.when(s == 0)
    def _():
        m_sc[...] = jnp.full(m_sc.shape, -jnp.inf, F32)
        l_sc[...] = jnp.zeros(l_sc.shape, F32)
        acc_sc[...] = jnp.zeros(acc_sc.shape, F32)

    def update(logits, v):
        m_prev = m_sc[...]
        m_new = jnp.maximum(m_prev, jnp.max(logits, axis=1, keepdims=True))
        alpha = jnp.exp2(m_prev - m_new)
        pr = jnp.exp2(logits - m_new)
        l_sc[...] = alpha * l_sc[...] + jnp.sum(pr, axis=1, keepdims=True)
        acc_sc[...] = alpha * acc_sc[...] + jnp.dot(pr.astype(BF16), v, preferred_element_type=F32)
        m_sc[...] = m_new

    q = q_ref[...]
    cq = cq_ref[...]
    _stage_pages(k_refs, v_refs, kb, vb, page, n_kv)
    ck = jnp.concatenate([c_ref[...]] * dec_seq, axis=0)
    update(_dot_nt(q, kb[...]) + (cq - ck) * LOG2E, vb[...])

    @pl.when(s == pl.num_programs(1) - 1)
    def _():
        rows = q.shape[0]
        ckn = jnp.concatenate([cnew_ref[...]] * dec_seq, axis=0)
        logits = _dot_nt(q, kn_ref[...].astype(BF16)) + (cq - ckn) * LOG2E
        tok = _row_token((rows, LANE), n_heads, dec_seq)
        logits = jnp.where(lax.broadcasted_iota(jnp.int32, (rows, LANE), 1) <= tok, logits, -jnp.inf)
        update(logits, vn_ref[...].astype(BF16))
        o_ref[...] = acc_sc[...] / l_sc[...]


def _moba_sample_body(pt_ref, *refs, pg, page, dec_seq, n_heads, n_kv, past):
    k_refs, v_refs = refs[:pg], refs[pg:2 * pg]
    (q_ref, q32_ref, slope_ref, kn_ref, vn_ref, o_ref,
     kb, vb, acc_all, m_all, l_all, km) = refs[2 * pg:]
    s = pl.program_id(1)
    nblk = past // MOBA_BLOCK
    ppb = MOBA_BLOCK // page
    rows = q_ref.shape[0]
    lane = lax.broadcasted_iota(jnp.int32, (rows, LANE), 1)
    qpos = (past + _row_token((rows, 1), n_heads, dec_seq)).astype(F32)
    km_row = lax.broadcasted_iota(jnp.int32, km.shape, 0)

    @pl.when(s == 0)
    def _():
        m_all[...] = jnp.full(m_all.shape, -jnp.inf, F32)
        l_all[...] = jnp.zeros(l_all.shape, F32)
        km[...] = jnp.zeros(km.shape, F32)

    def block_partial(logits, v, blk):
        mj = jnp.max(logits, axis=1, keepdims=True)
        pr = jnp.exp2(logits - mj)
        acc_all[blk] = jnp.dot(pr.astype(BF16), v, preferred_element_type=F32)
        m_all[...] = jnp.where(lane == blk, mj, m_all[...])
        l_all[...] = jnp.where(lane == blk, jnp.sum(pr, axis=1, keepdims=True), l_all[...])

    q = q_ref[...]
    slope = slope_ref[...]
    ksums = _stage_pages(k_refs, v_refs, kb, vb, page, n_kv, want_ksum=True)
    width = pg * page
    kpos = (s * width + lax.broadcasted_iota(jnp.int32, (1, width), 1)).astype(F32)
    logits = _dot_nt(q, kb[...]) - slope * (qpos - kpos)
    for jb in range(pg // ppb):
        blk = s * (pg // ppb) + jb
        block_partial(logits[:, jb * MOBA_BLOCK:(jb + 1) * MOBA_BLOCK],
                      vb[jb * MOBA_BLOCK:(jb + 1) * MOBA_BLOCK, :], blk)
        ksum = ksums[jb * ppb]
        for i in range(1, ppb):
            ksum = ksum + ksums[jb * ppb + i]
        km[...] = jnp.where(km_row == blk, ksum * (1.0 / MOBA_BLOCK), km[...])

    @pl.when(s == pl.num_programs(1) - 1)
    def _():
        npos = (past + lax.broadcasted_iota(jnp.int32, (1, LANE), 1)).astype(F32)
        ln = _dot_nt(q, kn_ref[...].astype(BF16)) - slope * (qpos - npos)
        ln = jnp.where(npos <= qpos, ln, -jnp.inf)
        block_partial(ln, vn_ref[...].astype(BF16), nblk)
        gate = lax.dot_general(q32_ref[...], km[...], (((1,), (1,)), ((), ())),
                               preferred_element_type=F32, precision=lax.Precision.HIGHEST)
        allowed = _top3_allowed(gate, lane, jnp.full((rows, LANE), nblk, jnp.int32)) > 0.0
        m_row = jnp.max(jnp.where(allowed, m_all[...], -jnp.inf), axis=1, keepdims=True)
        w = jnp.where(allowed, jnp.exp2(m_all[...] - m_row), 0.0)
        denom = jnp.sum(w * l_all[...], axis=1, keepdims=True)
        out = jnp.zeros(o_ref.shape, F32)
        for jb in range(nblk + 1):
            out = out + w[:, jb:jb + 1] * acc_all[jb]
        o_ref[...] = out / denom


def _sample_attention(page_table, cache_k4, cache_v4, layer, qbd, kn_pad, vn_pad, extra, *, moba,
                      dec_seq, n_heads):
    db, n_pages = page_table.shape
    kd = qbd.shape[2]
    hd = cache_k4.shape[3]
    n_kv = kd // hd
    page = cache_k4.shape[2] // n_kv
    rows = qbd.shape[1]
    past = n_pages * page
    pg = math.gcd(PAGES_PER_STEP, n_pages)
    assert page == LANE and MOBA_BLOCK % page == 0 and pg % (MOBA_BLOCK // page) == 0
    assert past % MOBA_BLOCK == 0 and past // MOBA_BLOCK < LANE
    page_specs = (_page_specs(pg, (None, None, page * n_kv, hd), layer, n_pages, pg)
                  + _page_specs(pg, (None, None, page * n_kv, hd), layer, n_pages, pg))
    per_b = lambda a: pl.BlockSpec((None,) + a.shape[1:], lambda b, s, pt: (b,) + (0,) * (a.ndim - 1))
    common = [pltpu.VMEM((pg * page, kd), BF16), pltpu.VMEM((pg * page, kd), BF16)]
    if moba:
        q32, slope = extra
        ins = [qbd, q32, slope, kn_pad, vn_pad]
        in_specs = page_specs + [per_b(qbd), per_b(q32),
                                 pl.BlockSpec(slope.shape, lambda b, s, pt: (0, 0)),
                                 per_b(kn_pad), per_b(vn_pad)]
        scratch = common + [pltpu.VMEM((past // MOBA_BLOCK + 1, rows, kd), F32),
                            pltpu.VMEM((rows, LANE), F32), pltpu.VMEM((rows, LANE), F32),
                            pltpu.VMEM((LANE, kd), F32)]
        body = functools.partial(_moba_sample_body, pg=pg, page=page, dec_seq=dec_seq,
                                 n_heads=n_heads, n_kv=n_kv, past=past)
    else:
        cq, c_t, cnew_t = extra
        ins = [qbd, cq, c_t, cnew_t, kn_pad, vn_pad]
        in_specs = page_specs + [per_b(qbd), per_b(cq),
                                 pl.BlockSpec((None, n_heads, pg * page), lambda b, s, pt: (b, 0, s)),
                                 per_b(cnew_t), per_b(kn_pad), per_b(vn_pad)]
        scratch = common + [pltpu.VMEM((rows, 1), F32), pltpu.VMEM((rows, 1), F32),
                            pltpu.VMEM((rows, kd), F32)]
        body = functools.partial(_fox_sample_body, pg=pg, page=page, dec_seq=dec_seq, n_heads=n_heads,
                                 n_kv=n_kv)
    grid_spec = pltpu.PrefetchScalarGridSpec(
        num_scalar_prefetch=1, grid=(db, n_pages // pg), in_specs=in_specs,
        out_specs=pl.BlockSpec((None, rows, kd), lambda b, s, pt: (b, 0, 0)), scratch_shapes=scratch)
    return pl.pallas_call(
        body, grid_spec=grid_spec, out_shape=jax.ShapeDtypeStruct((db, rows, kd), F32),
        compiler_params=_cparams(("arbitrary", "arbitrary")),
        name="moba_sample" if moba else "fox_sample")(
            page_table.reshape(-1), *([cache_k4] * pg), *([cache_v4] * pg), *ins)


def _wo_body(x_ref, mix_ref, w_ref, o_ref):
    o_ref[...] = x_ref[...] + jnp.dot(mix_ref[...], w_ref[...], preferred_element_type=F32)


def _wo_residual(x, mix_bf, wo_bf, layer, *, tm):
    rows, d = x.shape
    tm = _row_tile(rows, tm)
    return pl.pallas_call(
        _wo_body, grid=(rows // tm,),
        in_specs=[pl.BlockSpec((tm, d), lambda i: (i, 0)),
                  pl.BlockSpec((tm, mix_bf.shape[1]), lambda i: (i, 0)),
                  pl.BlockSpec((None,) + wo_bf.shape[1:], lambda i: (layer, 0, 0))],
        out_specs=pl.BlockSpec((tm, d), lambda i: (i, 0)),
        out_shape=jax.ShapeDtypeStruct((rows, d), F32),
        compiler_params=_cparams(("arbitrary",)), name="wo_residual")(x, mix_bf, wo_bf)


def _mlp_body(h_ref, g_ref, wu_ref, wd_ref, o_ref, hn_sc):
    f = pl.program_id(1)

    @pl.when(f == 0)
    def _():
        h = h_ref[...]
        ms = jnp.mean(h * h, axis=-1, keepdims=True)
        hn_sc[...] = (h * lax.rsqrt(ms + RMS_EPS) * g_ref[...]).astype(BF16)
        o_ref[...] = h

    u = jnp.maximum(jnp.dot(hn_sc[...], wu_ref[...], preferred_element_type=F32), 0.0)
    o_ref[...] += jnp.dot((u * u).astype(BF16), wd_ref[...], preferred_element_type=F32)


def _mlp_residual(h, g_mlp, w_up, w_down, layer, *, tm, tf):
    rows, d = h.shape
    dff = w_up.shape[2]
    tm = _row_tile(rows, tm)
    tf = min(tf, dff)
    assert dff % tf == 0
    return pl.pallas_call(
        _mlp_body, grid=(rows // tm, dff // tf),
        in_specs=[pl.BlockSpec((tm, d), lambda i, f: (i, 0)),
                  pl.BlockSpec(g_mlp.shape, lambda i, f: (0, 0)),
                  pl.BlockSpec((None, d, tf), lambda i, f: (layer, 0, f)),
                  pl.BlockSpec((None, tf, d), lambda i, f: (layer, f, 0))],
        out_specs=pl.BlockSpec((tm, d), lambda i, f: (i, 0)),
        out_shape=jax.ShapeDtypeStruct((rows, d), F32),
        scratch_shapes=[pltpu.VMEM((tm, d), BF16)],
        compiler_params=_cparams(("arbitrary", "arbitrary")),
        name="mlp_residual")(h, g_mlp, w_up, w_down)


def kernel(x_prompt, x_sample, cache_k, cache_v, cache_logf, page_table, attn_norm, w_qkv, q_norm, k_norm,
           w_fgate, b_fgate, w_o, mlp_norm, w_up, w_down):
    batch, seq, d = x_prompt.shape
    db, dec_seq, _ = x_sample.shape
    depth, n_pool, page, n_kv, hd = cache_k.shape
    n_heads = w_fgate.shape[2]
    group = n_heads // n_kv
    qd, kd = n_heads * hd, n_kv * hd
    n_pages = page_table.shape[1]
    past = n_pages * page
    rows_s = dec_seq * n_heads

    hp = x_prompt.reshape(batch * seq, d)
    hs = x_sample.reshape(db * dec_seq, d)
    cache_k4 = cache_k.reshape(depth, n_pool, page * n_kv, hd)
    cache_v4 = cache_v.reshape(depth, n_pool, page * n_kv, hd)
    row_kv = (jnp.arange(rows_s) % n_heads) // group
    row_mask = (row_kv[:, None] == jnp.arange(n_kv)[None, :]).astype(F32)
    slopes = jnp.exp2(-8.0 * jnp.arange(1, n_heads + 1, dtype=F32) / n_heads)

    def block_diag(q2d, dtype):
        q = q2d.astype(F32).reshape(db, rows_s, 1, hd) * row_mask[None, :, :, None]
        return q.reshape(db, rows_s, kd).astype(dtype)

    def pad_page(a2d):
        a = a2d.reshape(db, dec_seq, kd)
        return jnp.pad(a, ((0, 0), (0, LANE - dec_seq), (0, 0)))

    def own_lanes(o):
        o6 = o.reshape(db, dec_seq, n_kv, group, n_kv, hd)
        return jnp.stack([o6[:, :, k, :, k, :] for k in range(n_kv)], axis=2).reshape(db * dec_seq, qd)

    outs = {k: [] for k in ("kp", "vp", "fp", "ks", "vs", "fs")}
    w_qkv_bf, wo_bf = w_qkv.astype(BF16), w_o.astype(BF16)
    w_up_bf, w_down_bf = w_up.astype(BF16), w_down.astype(BF16)
    for i in range(depth):
        fox = i % 2 == 0
        j = i // 2
        g_attn, qg, kg = attn_norm[i][None, :], q_norm[i][None, :], k_norm[i][None, :]
        safe = _fixed_max_is_safe(qg, kg, hd)
        if fox:
            wf_bf = jnp.pad(w_fgate[j], ((0, 0), (0, LANE - n_heads))).astype(BF16)
            bf_pad = jnp.pad(b_fgate[j], (0, LANE - n_heads))[None, :]
        else:
            wf_bf = bf_pad = None
        proj = functools.partial(_qkv_proj, g_attn=g_attn, w_bf=w_qkv_bf, q_gain=qg, k_gain=kg, wf_bf=wf_bf,
                                 bf_pad=bf_pad, layer=i, n_heads=n_heads, n_kv=n_kv, hd=hd, q_f32=not fox)
        res_p = proj(hp, attn_copies=True, tm=512)
        res_s = proj(hs, attn_copies=False, tm=128)
        qp, kp32, vp32, kpb, vpt = res_p[:5]
        qs, ks32, vs32 = res_s[:3]
        outs["kp"].append(kp32.reshape(batch, seq, n_kv, hd))
        outs["vp"].append(vp32.reshape(batch, seq, n_kv, hd))
        outs["ks"].append(ks32.reshape(db, dec_seq, n_kv, hd))
        outs["vs"].append(vs32.reshape(db, dec_seq, n_kv, hd))
        kn_pad, vn_pad = pad_page(ks32), pad_page(vs32)
        if fox:
            lfp, lfp16, lfs16 = res_p[5], res_p[6], res_s[4]
            outs["fp"].append(lfp16.reshape(batch, seq, n_heads))
            outs["fs"].append(lfs16.reshape(db, dec_seq, n_heads))
            qx, kx = _fox_aug(lfp, batch=batch, seq=seq, n_heads=n_heads, n_kv=n_kv)
            mix_p = _flash_prompt(qp, qx, kpb, kx, vpt, None, safe, batch=batch, seq=seq, n_heads=n_heads,
                                  n_kv=n_kv, hd=hd, moba=False)
            logf_t = jnp.swapaxes(cache_logf[j], 1, 2)
            lfnew_t = jnp.pad(jnp.swapaxes(lfs16.reshape(db, dec_seq, n_heads), 1, 2),
                              ((0, 0), (0, 0), (0, LANE - dec_seq)))
            c_t, cnew_t = _cum_sample(page_table, logf_t, lfnew_t, n_heads=n_heads)
            cq = jnp.swapaxes(cnew_t[:, :, :dec_seq], 1, 2).reshape(db, rows_s, 1)
            o_s = _sample_attention(page_table, cache_k4, cache_v4, i, block_diag(qs, BF16), kn_pad, vn_pad,
                                    (cq, c_t, cnew_t), moba=False, dec_seq=dec_seq, n_heads=n_heads)
        else:
            kmean = _kmean(kp32, batch=batch, seq=seq, n_kv=n_kv)
            qx, kx = _moba_static_cols(seq, n_heads)
            mix_p = _flash_prompt(qp, qx, kpb, kx, vpt, kmean, safe, batch=batch, seq=seq, n_heads=n_heads,
                                  n_kv=n_kv, hd=hd, moba=True)
            slope_col = (jnp.tile(slopes, dec_seq) * LOG2E)[:, None]
            q_scaled = qs * (hd ** -0.5 * LOG2E)
            o_s = _sample_attention(page_table, cache_k4, cache_v4, i, block_diag(q_scaled, BF16), kn_pad,
                                    vn_pad, (block_diag(qs, F32), slope_col), moba=True, dec_seq=dec_seq,
                                    n_heads=n_heads)
        mix_s = own_lanes(o_s).astype(BF16)
        g_mlp = mlp_norm[i][None, :]
        hp = _wo_residual(hp, mix_p, wo_bf, i, tm=512)
        hs = _wo_residual(hs, mix_s, wo_bf, i, tm=128)
        hp = _mlp_residual(hp, g_mlp, w_up_bf, w_down_bf, i, tm=1024, tf=512)
        hs = _mlp_residual(hs, g_mlp, w_up_bf, w_down_bf, i, tm=128, tf=512)
    return (hp.reshape(batch, seq, d), hs.reshape(db, dec_seq, d),
            jnp.stack(outs["kp"]), jnp.stack(outs["vp"]), jnp.stack(outs["fp"]),
            jnp.stack(outs["ks"]), jnp.stack(outs["vs"]), jnp.stack(outs["fs"]))
```

```python
import functools
import math

import numpy as np
import jax
import jax.numpy as jnp
from jax import lax
from jax.experimental import pallas as pl
from jax.experimental.pallas import tpu as pltpu

F32 = jnp.float32
BF16 = jnp.bfloat16
LOG2E = 1.4426950408889634
RMS_EPS = 1e-6
MOBA_BLOCK = 256
MOBA_TOPK = 3
LANE = 128
ATTN_TILE = 512
FLASH_CHUNK = 512
DENOM_ROWS = 16
FIXED_MAX_HEADROOM = 64.0
PAGES_PER_STEP = 32
NEG_BIG = -(2.0 ** 100)
VMEM_LIMIT = 56 * 1024 * 1024
COL_BLK = 0
COL_QPOS = 64
COL_KPOS = 67


def _cparams(sem):
    return pltpu.CompilerParams(dimension_semantics=sem, vmem_limit_bytes=VMEM_LIMIT)


def _row_tile(rows, pref):
    t = min(rows, pref)
    while rows % t:
        t -= 8
    return t


def _split3(x):
    hi = x.astype(BF16).astype(F32)
    r1 = x - hi
    mid = r1.astype(BF16).astype(F32)
    lo = (r1 - mid).astype(BF16).astype(F32)
    return hi, mid, lo


def _dot_nt(a, b):
    return lax.dot_general(a, b, (((1,), (1,)), ((), ())), preferred_element_type=F32)


def _qkv_body(*refs, n_heads, n_kv, hd, q_scale, with_gate, q_f32, attn_copies):
    n_in = 7 if with_gate else 5
    x_ref, g_ref, w_ref, qg_ref, kg_ref = refs[:5]
    outs = list(refs[n_in:])
    q_ref, k32_ref, v32_ref = outs[:3]
    del outs[:3]
    if attn_copies:
        kb_ref, vt_ref = outs[:2]
        del outs[:2]
    qd, kd = n_heads * hd, n_kv * hd
    x = x_ref[...]
    ms = jnp.mean(x * x, axis=-1, keepdims=True)
    xn = (x * lax.rsqrt(ms + RMS_EPS) * g_ref[...]).astype(BF16)

    def head_norm(a, gain):
        m = jnp.mean(a * a, axis=-1, keepdims=True)
        return a * lax.rsqrt(m + RMS_EPS) * gain

    chunk = 4 * hd
    for c in range(qd // chunk):
        acc = jnp.dot(xn, w_ref[:, c * chunk:(c + 1) * chunk], preferred_element_type=F32)
        for j in range(4):
            qn = head_norm(acc[:, j * hd:(j + 1) * hd], qg_ref[...])
            lo = c * chunk + j * hd
            if q_f32:
                q_ref[:, lo:lo + hd] = qn
            else:
                q_ref[:, lo:lo + hd] = (qn * q_scale).astype(BF16)
    tm = x.shape[0]
    acc = jnp.dot(xn, w_ref[:, qd:qd + kd], preferred_element_type=F32)
    for j in range(n_kv):
        kn = head_norm(acc[:, j * hd:(j + 1) * hd], kg_ref[...])
        k32_ref[pl.ds(j, tm, stride=n_kv), :] = kn
        if attn_copies:
            kb_ref[:, j * hd:(j + 1) * hd] = kn.astype(BF16)
    acc = jnp.dot(xn, w_ref[:, qd + kd:qd + 2 * kd], preferred_element_type=F32)
    for j in range(n_kv):
        v32_ref[pl.ds(j, tm, stride=n_kv), :] = acc[:, j * hd:(j + 1) * hd]
    if attn_copies:
        for j in range(n_kv):
            vt_ref[j] = acc[:, j * hd:(j + 1) * hd].T.astype(BF16)
    if with_gate:
        wf_ref, bf_ref = refs[5:7]
        z = jnp.dot(xn, wf_ref[...], preferred_element_type=F32) + bf_ref[...]
        lf = jnp.minimum(z, 0.0) - jnp.log1p(jnp.exp(-jnp.abs(z)))
        outs[0][...] = lf
        outs[1][...] = lf[:, :n_heads]


def _qkv_proj(x, g_attn, w_bf, q_gain, k_gain, wf_bf, bf_pad, *, layer, n_heads, n_kv, hd, q_f32,
              attn_copies, tm):
    rows, d = x.shape
    qd, kd = n_heads * hd, n_kv * hd
    with_gate = wf_bf is not None
    tm = _row_tile(rows, tm)
    row_spec = lambda w: pl.BlockSpec((tm, w), lambda i: (i, 0))
    full = lambda a: pl.BlockSpec(a.shape, lambda i: (0,) * a.ndim)
    ins = [x, g_attn, w_bf, q_gain, k_gain]
    in_specs = [row_spec(d), full(g_attn), pl.BlockSpec((None,) + w_bf.shape[1:], lambda i: (layer, 0, 0)),
                full(q_gain), full(k_gain)]
    kv_spec = pl.BlockSpec((tm * n_kv, hd), lambda i: (i, 0))
    out_shape = [jax.ShapeDtypeStruct((rows, qd), F32 if q_f32 else BF16),
                 jax.ShapeDtypeStruct((rows * n_kv, hd), F32), jax.ShapeDtypeStruct((rows * n_kv, hd), F32)]
    out_specs = [row_spec(qd), kv_spec, kv_spec]
    if attn_copies:
        assert tm == ATTN_TILE
        out_shape += [jax.ShapeDtypeStruct((rows, kd), BF16),
                      jax.ShapeDtypeStruct((n_kv, rows // tm, hd, tm), BF16)]
        out_specs += [row_spec(kd), pl.BlockSpec((n_kv, None, hd, tm), lambda i: (0, i, 0, 0))]
    if with_gate:
        ins += [wf_bf, bf_pad]
        in_specs += [full(wf_bf), full(bf_pad)]
        out_shape += [jax.ShapeDtypeStruct((rows, LANE), F32), jax.ShapeDtypeStruct((rows, n_heads), F32)]
        out_specs += [row_spec(LANE), row_spec(n_heads)]
    body = functools.partial(_qkv_body, n_heads=n_heads, n_kv=n_kv, hd=hd,
                             q_scale=hd ** -0.5 * LOG2E, with_gate=with_gate, q_f32=q_f32,
                             attn_copies=attn_copies)
    return pl.pallas_call(body, grid=(rows // tm,), in_specs=in_specs, out_specs=out_specs,
                          out_shape=out_shape, compiler_params=_cparams(("arbitrary",)),
                          name="qkv_proj")(*ins)


def _fox_aug_body(lf_ref, qx_ref, kx_ref, carry_ref, *, n_heads, n_kv):
    t = pl.program_id(1)

    @pl.when(t == 0)
    def _():
        carry_ref[...] = jnp.zeros_like(carry_ref)

    lf = lf_ref[...]
    tc = lf.shape[0]
    tri = (lax.broadcasted_iota(jnp.int32, (tc, tc), 1)
           <= lax.broadcasted_iota(jnp.int32, (tc, tc), 0)).astype(F32)
    c = jnp.dot(tri, lf, preferred_element_type=F32, precision=lax.Precision.HIGHEST) + carry_ref[...]
    carry_ref[...] = c[tc - 1:tc, :]
    c2 = c * LOG2E
    lane = lax.broadcasted_iota(jnp.int32, (tc, LANE), 1)
    group = n_heads // n_kv
    ones3 = jnp.where(lane < 3, 1.0, 0.0)
    for kvh in range(n_kv):
        kaug = ones3
        for g in range(group):
            h = kvh * group + g
            hi, mid, lo = _split3(jnp.broadcast_to(c2[:, h:h + 1], (tc, LANE)))
            base = 3 + 3 * g
            own = jnp.where((lane >= base) & (lane < base + 3), 1.0, 0.0)
            qaug = jnp.where(lane == 0, hi, jnp.where(lane == 1, mid, jnp.where(lane == 2, lo, own)))
            qx_ref[:, h * LANE:(h + 1) * LANE] = qaug.astype(BF16)
            kaug = jnp.where(lane == base, -hi,
                             jnp.where(lane == base + 1, -mid, jnp.where(lane == base + 2, -lo, kaug)))
        kx_ref[:, kvh * LANE:(kvh + 1) * LANE] = kaug.astype(BF16)


def _fox_aug(lf128, *, batch, seq, n_heads, n_kv):
    tc = _row_tile(seq, 512)
    nt = seq // tc
    body = functools.partial(_fox_aug_body, n_heads=n_heads, n_kv=n_kv)
    return pl.pallas_call(
        body, grid=(batch, nt),
        in_specs=[pl.BlockSpec((tc, LANE), lambda b, t: (b * nt + t, 0))],
        out_specs=[pl.BlockSpec((tc, n_heads * LANE), lambda b, t: (b * nt + t, 0)),
                   pl.BlockSpec((tc, n_kv * LANE), lambda b, t: (b * nt + t, 0))],
        out_shape=[jax.ShapeDtypeStruct((batch * seq, n_heads * LANE), BF16),
                   jax.ShapeDtypeStruct((batch * seq, n_kv * LANE), BF16)],
        scratch_shapes=[pltpu.VMEM((1, LANE), F32)],
        compiler_params=_cparams(("arbitrary", "arbitrary")), name="fox_aug")(lf128)


def _kmean_body(k_ref, o_ref, *, nblk, n_kv):
    hd = k_ref.shape[1]
    for j in range(nblk):
        for h in range(n_kv):
            blk = k_ref[pl.ds(j * MOBA_BLOCK * n_kv + h, MOBA_BLOCK, stride=n_kv), :]
            o_ref[j:j + 1, h * hd:(h + 1) * hd] = jnp.sum(blk, axis=0, keepdims=True) * (1.0 / MOBA_BLOCK)
    if o_ref.shape[0] > nblk:
        o_ref[nblk:, :] = jnp.zeros((o_ref.shape[0] - nblk, o_ref.shape[1]), F32)


def _kmean(k32, *, batch, seq, n_kv):
    hd = k32.shape[1]
    nblk = seq // MOBA_BLOCK
    assert nblk <= COL_QPOS
    return pl.pallas_call(
        functools.partial(_kmean_body, nblk=nblk, n_kv=n_kv), grid=(batch,),
        in_specs=[pl.BlockSpec((seq * n_kv, hd), lambda b: (b, 0))],
        out_specs=pl.BlockSpec((None, LANE, n_kv * hd), lambda b: (b, 0, 0)),
        out_shape=jax.ShapeDtypeStruct((batch, LANE, n_kv * hd), F32),
        compiler_params=_cparams(("arbitrary",)), name="moba_kmean")(k32)


def _top3_allowed(gate, lane, own):
    lane_f = lane.astype(F32)
    gm = jnp.where(lane < own, gate, -jnp.inf)
    sel = jnp.zeros(gate.shape, F32)
    for _ in range(MOBA_TOPK):
        mx = jnp.max(gm, axis=1, keepdims=True)
        idx = jnp.min(jnp.where(gm == mx, lane_f, float(LANE)), axis=1, keepdims=True)
        pick = lane_f == idx
        sel = jnp.where(pick, 1.0, sel)
        gm = jnp.where(pick, -jnp.inf, gm)
    return jnp.where(lane < own, sel, jnp.where(lane == own, 1.0, 0.0))


def _fixed_max_is_safe(q_gain, k_gain, hd):
    b = math.sqrt(hd) * LOG2E * jnp.max(jnp.abs(q_gain)) * jnp.max(jnp.abs(k_gain))
    return (2.0 * 1.02 * b + 2.0 < FIXED_MAX_HEADROOM).astype(jnp.int32).reshape(1)


def _flash_body(safe_ref, *refs, group, hd, moba, tile, nblk):
    if moba:
        q_ref, qx_ref, k_ref, kx_ref, vt_ref, km_ref, o_ref, qaug, m_sc, acc_sc, qlo = refs
    else:
        q_ref, qx_ref, k_ref, kx_ref, vt_ref, o_ref, qaug, m_sc, acc_sc = refs
    i = pl.program_id(2)

    def moba_selection_bias():
        nbr = -(-nblk // 16) * 16
        km = km_ref[:nbr, :]
        km_hi = km.astype(BF16)
        km_lo = (km - km_hi.astype(F32)).astype(BF16)
        g1 = _dot_nt(jnp.concatenate([km_hi, km_lo], axis=0), qaug[:, :hd])
        gate = g1[:nbr, :] + g1[nbr:, :] + _dot_nt(km_hi, qlo[...])
        blk = lax.broadcasted_iota(jnp.int32, gate.shape, 0)
        pos = i * tile + (lax.broadcasted_iota(jnp.int32, gate.shape, 1) & (tile - 1))
        own = lax.shift_right_logical(pos, int(math.log2(MOBA_BLOCK)))
        blk_f = blk.astype(F32)
        gm = jnp.where(blk < own, gate, -jnp.inf)
        sel = jnp.zeros(gate.shape, F32)
        for _ in range(MOBA_TOPK):
            mx = jnp.max(gm, axis=0, keepdims=True)
            idx = jnp.min(jnp.where(gm == mx, blk_f, float(LANE)), axis=0, keepdims=True)
            pick = blk_f == idx
            sel = jnp.where(pick, 1.0, sel)
            gm = jnp.where(pick, -jnp.inf, gm)
        allowed = jnp.where(blk < own, sel, jnp.where(blk == own, 1.0, 0.0))
        bias = jnp.where(allowed > 0.0, 0.0, NEG_BIG)
        return jnp.concatenate([bias, jnp.zeros((LANE - nbr, gate.shape[1]), F32)], axis=0)

    for g in range(group):
        rows = slice(g * tile, (g + 1) * tile)
        if moba:
            qs = q_ref[:, g * hd:(g + 1) * hd] * (hd ** -0.5 * LOG2E)
            hi = qs.astype(BF16)
            qaug[rows, :hd] = hi
            qlo[rows, :] = (qs - hi.astype(F32)).astype(BF16)
        else:
            qaug[rows, :hd] = q_ref[:, g * hd:(g + 1) * hd]
            qaug[rows, hd:] = qx_ref[:, g * LANE:(g + 1) * LANE]
    if moba:
        bias_t = moba_selection_bias()
        lane = lax.broadcasted_iota(jnp.int32, (tile, LANE), 1)
        for g in range(group):
            aug = jnp.where(lane < COL_QPOS, bias_t[:, g * tile:(g + 1) * tile].T,
                            qx_ref[:, g * LANE:(g + 1) * LANE].astype(F32))
            qaug[g * tile:(g + 1) * tile, hd:] = aug.astype(BF16)
    m_sc[...] = jnp.full(m_sc.shape, -jnp.inf, F32)
    acc_sc[...] = jnp.zeros(acc_sc.shape, F32)

    def step(j, diagonal, fixed_max=False):
        keys = pl.ds(pl.multiple_of(j * tile, tile), tile)
        kaug = jnp.concatenate([k_ref[keys, :], kx_ref[keys, :]], axis=1)
        vt = jnp.concatenate([vt_ref[j], jnp.ones((DENOM_ROWS, tile), BF16)], axis=0)
        n_chunks = group * tile // FLASH_CHUNK
        scores = [_dot_nt(kaug, qaug[0:FLASH_CHUNK, :])]
        for c in range(n_chunks):
            cols = slice(c * FLASH_CHUNK, (c + 1) * FLASH_CHUNK)
            if c + 1 < n_chunks:
                scores.append(_dot_nt(kaug, qaug[(c + 1) * FLASH_CHUNK:(c + 2) * FLASH_CHUNK, :]))
            st = scores[c]
            if fixed_max:
                pt = jnp.exp2(st - m_sc[:, cols])
                acc_sc[:, cols] += jnp.dot(vt, pt.astype(BF16), preferred_element_type=F32)
                continue
            if diagonal:
                key = lax.broadcasted_iota(jnp.int32, st.shape, 0)
                query = (c * FLASH_CHUNK) % tile + lax.broadcasted_iota(jnp.int32, st.shape, 1)
                st = jnp.where(key <= query, st, -jnp.inf)
            m_prev = m_sc[:, cols]
            m_new = jnp.maximum(m_prev, jnp.max(st, axis=0, keepdims=True))
            alpha = jnp.exp2(m_prev - m_new)
            pt = jnp.exp2(st - m_new)
            acc_sc[:, cols] = alpha * acc_sc[:, cols] + jnp.dot(vt, pt.astype(BF16),
                                                                preferred_element_type=F32)
            m_sc[:, cols] = m_new

    step(i, True)
    fixed = safe_ref[0] == 1

    @pl.when(fixed)
    def _():
        lax.fori_loop(0, i, lambda j, carry: (step(j, False, fixed_max=True), carry)[1], 0)

    @pl.when(jnp.logical_not(fixed))
    def _():
        lax.fori_loop(0, i, lambda j, carry: (step(j, False), carry)[1], 0)

    out_t = acc_sc[:hd, :] / acc_sc[hd:hd + 1, :]
    for g in range(group):
        o_ref[:, g * hd:(g + 1) * hd] = out_t[:, g * tile:(g + 1) * tile].T.astype(o_ref.dtype)


def _flash_prompt(q, qx, kb, kx, vt, kmean, safe, *, batch, seq, n_heads, n_kv, hd, moba):
    tile = ATTN_TILE
    assert seq % tile == 0 and tile % MOBA_BLOCK == 0 and tile & (tile - 1) == 0
    nt = seq // tile
    group = n_heads // n_kv
    gw = group * hd
    q_map = lambda b, h, i, *_: (b * nt + i, h)
    seq_map = lambda b, h, i, *_: (b, h)
    vt_spec = pl.BlockSpec((None, nt, hd, tile), lambda b, h, i, *_: (h, b, 0, 0))
    in_specs = [pl.BlockSpec((tile, gw), q_map)]
    if moba:
        in_specs += [pl.BlockSpec((tile, group * LANE), lambda b, h, i, *_: (i, h)),
                     pl.BlockSpec((seq, hd), seq_map),
                     pl.BlockSpec((seq, LANE), lambda b, h, i, *_: (0, 0)),
                     vt_spec,
                     pl.BlockSpec((None, LANE, hd), lambda b, h, i, *_: (b, 0, h))]
        ins = [q, qx, kb, kx, vt, kmean]
    else:
        in_specs += [pl.BlockSpec((tile, group * LANE), q_map),
                     pl.BlockSpec((seq, hd), seq_map),
                     pl.BlockSpec((seq, LANE), seq_map),
                     vt_spec]
        ins = [q, qx, kb, kx, vt]
    body = functools.partial(_flash_body, group=group, hd=hd, moba=moba, tile=tile,
                             nblk=seq // MOBA_BLOCK)
    scratch = [pltpu.VMEM((group * tile, hd + LANE), BF16), pltpu.VMEM((1, group * tile), F32),
               pltpu.VMEM((hd + DENOM_ROWS, group * tile), F32)]
    if moba:
        scratch.append(pltpu.VMEM((group * tile, hd), BF16))
    grid_spec = pltpu.PrefetchScalarGridSpec(
        num_scalar_prefetch=1, grid=(batch, n_kv, nt), in_specs=in_specs,
        out_specs=pl.BlockSpec((tile, gw), q_map), scratch_shapes=scratch)
    return pl.pallas_call(
        body, grid_spec=grid_spec, out_shape=jax.ShapeDtypeStruct((batch * seq, n_heads * hd), BF16),
        compiler_params=_cparams(("arbitrary", "arbitrary", "arbitrary")),
        name="moba_prompt" if moba else "fox_prompt")(safe, *ins)


def _moba_static_cols(seq, n_heads):
    pos = jnp.arange(seq, dtype=jnp.int32)
    slopes = jnp.exp2(-8.0 * jnp.arange(1, n_heads + 1, dtype=F32) / n_heads) * LOG2E
    qparts = _split3(-(slopes[None, :] * pos.astype(F32)[:, None]))
    sparts = _split3(slopes)
    qx = jnp.zeros((seq, n_heads, LANE), F32)
    for c in range(3):
        qx = qx.at[:, :, COL_QPOS + c].set(qparts[c])
        qx = qx.at[:, :, COL_KPOS + c].set(jnp.broadcast_to(sparts[c][None, :], (seq, n_heads)))
        qx = qx.at[:, :, COL_KPOS + 3 + c].set(jnp.broadcast_to(sparts[c][None, :], (seq, n_heads)))
    blk = pos // MOBA_BLOCK
    lane = jnp.arange(LANE, dtype=jnp.int32)[None, :]
    kx = jnp.where(lane == blk[:, None], 1.0, 0.0)
    kx = jnp.where((lane >= COL_QPOS) & (lane < COL_QPOS + 3), 1.0, kx)
    kx = jnp.where((lane >= COL_KPOS) & (lane < COL_KPOS + 3), (blk * MOBA_BLOCK).astype(F32)[:, None], kx)
    kx = jnp.where((lane >= COL_KPOS + 3) & (lane < COL_KPOS + 6), (pos % MOBA_BLOCK).astype(F32)[:, None], kx)
    return qx.reshape(seq, n_heads * LANE).astype(BF16), kx.astype(BF16)


def _page_specs(n, block, layer_arg, n_pages, pages_per_step):
    def spec(i):
        def imap(b, s, pt):
            return (layer_arg, pt[b * n_pages + s * pages_per_step + i]) + (0,) * (len(block) - 2)
        return pl.BlockSpec(block, imap)
    return [spec(i) for i in range(n)]


def _cum_sample_body(pt_ref, *refs, pg):
    lf_refs, new_ref, c_ref, cnew_ref, carry_ref = refs[:pg], refs[pg], refs[pg + 1], refs[pg + 2], refs[pg + 3]
    s = pl.program_id(1)

    @pl.when(s == 0)
    def _():
        carry_ref[...] = jnp.zeros_like(carry_ref)

    triu = (lax.broadcasted_iota(jnp.int32, (LANE, LANE), 0)
            <= lax.broadcasted_iota(jnp.int32, (LANE, LANE), 1)).astype(F32)
    carry = carry_ref[...]
    nh = carry.shape[0]
    local = jnp.dot(jnp.concatenate([r[...] for r in lf_refs], axis=0), triu,
                    preferred_element_type=F32, precision=lax.Precision.HIGHEST)
    for i in range(pg):
        loc = local[i * nh:(i + 1) * nh, :]
        c_ref[:, i * LANE:(i + 1) * LANE] = loc + carry
        carry = carry + jnp.broadcast_to(loc[:, LANE - 1:LANE], loc.shape)
    carry_ref[...] = carry

    @pl.when(s == pl.num_programs(1) - 1)
    def _():
        cnew_ref[...] = jnp.dot(new_ref[...], triu, preferred_element_type=F32,
                                precision=lax.Precision.HIGHEST) + carry


def _cum_sample(page_table, logf_t, lfnew_t, *, n_heads):
    db, n_pages = page_table.shape
    page = logf_t.shape[2]
    assert page == LANE
    pg = math.gcd(PAGES_PER_STEP, n_pages)
    in_specs = _page_specs(pg, (None, None, n_heads, page), 0, n_pages, pg)
    in_specs.append(pl.BlockSpec((None, n_heads, LANE), lambda b, s, pt: (b, 0, 0)))
    grid_spec = pltpu.PrefetchScalarGridSpec(
        num_scalar_prefetch=1, grid=(db, n_pages // pg), in_specs=in_specs,
        out_specs=[pl.BlockSpec((None, n_heads, pg * page), lambda b, s, pt: (b, 0, s)),
                   pl.BlockSpec((None, n_heads, LANE), lambda b, s, pt: (b, 0, 0))],
        scratch_shapes=[pltpu.VMEM((n_heads, LANE), F32)])
    return pl.pallas_call(
        functools.partial(_cum_sample_body, pg=pg), grid_spec=grid_spec,
        out_shape=[jax.ShapeDtypeStruct((db, n_heads, n_pages * page), F32),
                   jax.ShapeDtypeStruct((db, n_heads, LANE), F32)],
        compiler_params=_cparams(("arbitrary", "arbitrary")),
        name="fox_sample_cumsum")(page_table.reshape(-1), *([logf_t[None]] * pg), lfnew_t)


def _row_token(shape, n_heads, dec_seq):
    row = lax.broadcasted_iota(jnp.int32, shape, 0)
    tok = jnp.zeros(shape, jnp.int32)
    for t in range(1, dec_seq):
        tok = tok + (row >= t * n_heads).astype(jnp.int32)
    return tok


def _stage_pages(k_refs, v_refs, kb, vb, page, n_kv, want_ksum=False):
    hd = kb.shape[1] // n_kv
    ksums = []
    for i, (k_ref, v_ref) in enumerate(zip(k_refs, v_refs)):
        parts = []
        for h in range(n_kv):
            kf = k_ref[pl.ds(h, page, stride=n_kv), :]
            kb[i * page:(i + 1) * page, h * hd:(h + 1) * hd] = kf.astype(BF16)
            vb[i * page:(i + 1) * page, h * hd:(h + 1) * hd] = (
                v_ref[pl.ds(h, page, stride=n_kv), :].astype(BF16))
            if want_ksum:
                parts.append(jnp.sum(kf, axis=0, keepdims=True))
        if want_ksum:
            ksums.append(jnp.concatenate(parts, axis=1))
    return ksums


def _fox_sample_body(pt_ref, *refs, pg, page, dec_seq, n_heads, n_kv):
    k_refs, v_refs = refs[:pg], refs[pg:2 * pg]
    (q_ref, cq_ref, c_ref, cnew_ref, kn_ref, vn_ref, o_ref, kb, vb, m_sc, l_sc, acc_sc) = refs[2 * pg:]
    s = pl.program_id(1)

    @pl.when(s == 0)
    def _():
        m_sc[...] = jnp.full(m_sc.shape, -jnp.inf, F32)
        l_sc[...] = jnp.zeros(l_sc.shape, F32)
        acc_sc[...] = jnp.zeros(acc_sc.shape, F32)

    def update(logits, v):
        m_prev = m_sc[...]
        m_new = jnp.maximum(m_prev, jnp.max(logits, axis=1, keepdims=True))
        alpha = jnp.exp2(m_prev - m_new)
        pr = jnp.exp2(logits - m_new)
        l_sc[...] = alpha * l_sc[...] + jnp.sum(pr, axis=1, keepdims=True)
        acc_sc[...] = alpha * acc_sc[...] + jnp.dot(pr.astype(BF16), v, preferred_element_type=F32)
        m_sc[...] = m_new

    q = q_ref[...]
    cq = cq_ref[...]
    _stage_pages(k_refs, v_refs, kb, vb, page, n_kv)
    ck = jnp.concatenate([c_ref[...]] * dec_seq, axis=0)
    update(_dot_nt(q, kb[...]) + (cq - ck) * LOG2E, vb[...])

    @pl.when(s == pl.num_programs(1) - 1)
    def _():
        rows = q.shape[0]
        ckn = jnp.concatenate([cnew_ref[...]] * dec_seq, axis=0)
        logits = _dot_nt(q, kn_ref[...].astype(BF16)) + (cq - ckn) * LOG2E
        tok = _row_token((rows, LANE), n_heads, dec_seq)
        logits = jnp.where(lax.broadcasted_iota(jnp.int32, (rows, LANE), 1) <= tok, logits, -jnp.inf)
        update(logits, vn_ref[...].astype(BF16))
        o_ref[...] = acc_sc[...] / l_sc[...]


def _moba_sample_body(pt_ref, *refs, pg, page, dec_seq, n_heads, n_kv, past):
    k_refs, v_refs = refs[:pg], refs[pg:2 * pg]
    (q_ref, q32_ref, slope_ref, kn_ref, vn_ref, o_ref,
     kb, vb, acc_all, m_all, l_all, km) = refs[2 * pg:]
    s = pl.program_id(1)
    nblk = past // MOBA_BLOCK
    ppb = MOBA_BLOCK // page
    rows = q_ref.shape[0]
    lane = lax.broadcasted_iota(jnp.int32, (rows, LANE), 1)
    qpos = (past + _row_token((rows, 1), n_heads, dec_seq)).astype(F32)
    km_row = lax.broadcasted_iota(jnp.int32, km.shape, 0)

    @pl.when(s == 0)
    def _():
        m_all[...] = jnp.full(m_all.shape, -jnp.inf, F32)
        l_all[...] = jnp.zeros(l_all.shape, F32)
        km[...] = jnp.zeros(km.shape, F32)

    def block_partial(logits, v, blk):
        mj = jnp.max(logits, axis=1, keepdims=True)
        pr = jnp.exp2(logits - mj)
        acc_all[blk] = jnp.dot(pr.astype(BF16), v, preferred_element_type=F32)
        m_all[...] = jnp.where(lane == blk, mj, m_all[...])
        l_all[...] = jnp.where(lane == blk, jnp.sum(pr, axis=1, keepdims=True), l_all[...])

    q = q_ref[...]
    slope = slope_ref[...]
    ksums = _stage_pages(k_refs, v_refs, kb, vb, page, n_kv, want_ksum=True)
    width = pg * page
    kpos = (s * width + lax.broadcasted_iota(jnp.int32, (1, width), 1)).astype(F32)
    logits = _dot_nt(q, kb[...]) - slope * (qpos - kpos)
    for jb in range(pg // ppb):
        blk = s * (pg // ppb) + jb
        block_partial(logits[:, jb * MOBA_BLOCK:(jb + 1) * MOBA_BLOCK],
                      vb[jb * MOBA_BLOCK:(jb + 1) * MOBA_BLOCK, :], blk)
        ksum = ksums[jb * ppb]
        for i in range(1, ppb):
            ksum = ksum + ksums[jb * ppb + i]
        km[...] = jnp.where(km_row == blk, ksum * (1.0 / MOBA_BLOCK), km[...])

    @pl.when(s == pl.num_programs(1) - 1)
    def _():
        npos = (past + lax.broadcasted_iota(jnp.int32, (1, LANE), 1)).astype(F32)
        ln = _dot_nt(q, kn_ref[...].astype(BF16)) - slope * (qpos - npos)
        ln = jnp.where(npos <= qpos, ln, -jnp.inf)
        block_partial(ln, vn_ref[...].astype(BF16), nblk)
        gate = lax.dot_general(q32_ref[...], km[...], (((1,), (1,)), ((), ())),
                               preferred_element_type=F32, precision=lax.Precision.HIGHEST)
        allowed = _top3_allowed(gate, lane, jnp.full((rows, LANE), nblk, jnp.int32)) > 0.0
        m_row = jnp.max(jnp.where(allowed, m_all[...], -jnp.inf), axis=1, keepdims=True)
        w = jnp.where(allowed, jnp.exp2(m_all[...] - m_row), 0.0)
        denom = jnp.sum(w * l_all[...], axis=1, keepdims=True)
        out = jnp.zeros(o_ref.shape, F32)
        for jb in range(nblk + 1):
            out = out + w[:, jb:jb + 1] * acc_all[jb]
        o_ref[...] = out / denom


def _sample_attention(page_table, cache_k4, cache_v4, layer, qbd, kn_pad, vn_pad, extra, *, moba,
                      dec_seq, n_heads):
    db, n_pages = page_table.shape
    kd = qbd.shape[2]
    hd = cache_k4.shape[3]
    n_kv = kd // hd
    page = cache_k4.shape[2] // n_kv
    rows = qbd.shape[1]
    past = n_pages * page
    pg = math.gcd(PAGES_PER_STEP, n_pages)
    assert page == LANE and MOBA_BLOCK % page == 0 and pg % (MOBA_BLOCK // page) == 0
    assert past % MOBA_BLOCK == 0 and past // MOBA_BLOCK < LANE
    page_specs = (_page_specs(pg, (None, None, page * n_kv, hd), layer, n_pages, pg)
                  + _page_specs(pg, (None, None, page * n_kv, hd), layer, n_pages, pg))
    per_b = lambda a: pl.BlockSpec((None,) + a.shape[1:], lambda b, s, pt: (b,) + (0,) * (a.ndim - 1))
    common = [pltpu.VMEM((pg * page, kd), BF16), pltpu.VMEM((pg * page, kd), BF16)]
    if moba:
        q32, slope = extra
        ins = [qbd, q32, slope, kn_pad, vn_pad]
        in_specs = page_specs + [per_b(qbd), per_b(q32),
                                 pl.BlockSpec(slope.shape, lambda b, s, pt: (0, 0)),
                                 per_b(kn_pad), per_b(vn_pad)]
        scratch = common + [pltpu.VMEM((past // MOBA_BLOCK + 1, rows, kd), F32),
                            pltpu.VMEM((rows, LANE), F32), pltpu.VMEM((rows, LANE), F32),
                            pltpu.VMEM((LANE, kd), F32)]
        body = functools.partial(_moba_sample_body, pg=pg, page=page, dec_seq=dec_seq,
                                 n_heads=n_heads, n_kv=n_kv, past=past)
    else:
        cq, c_t, cnew_t = extra
        ins = [qbd, cq, c_t, cnew_t, kn_pad, vn_pad]
        in_specs = page_specs + [per_b(qbd), per_b(cq),
                                 pl.BlockSpec((None, n_heads, pg * page), lambda b, s, pt: (b, 0, s)),
                                 per_b(cnew_t), per_b(kn_pad), per_b(vn_pad)]
        scratch = common + [pltpu.VMEM((rows, 1), F32), pltpu.VMEM((rows, 1), F32),
                            pltpu.VMEM((rows, kd), F32)]
        body = functools.partial(_fox_sample_body, pg=pg, page=page, dec_seq=dec_seq, n_heads=n_heads,
                                 n_kv=n_kv)
    grid_spec = pltpu.PrefetchScalarGridSpec(
        num_scalar_prefetch=1, grid=(db, n_pages // pg), in_specs=in_specs,
        out_specs=pl.BlockSpec((None, rows, kd), lambda b, s, pt: (b, 0, 0)), scratch_shapes=scratch)
    return pl.pallas_call(
        body, grid_spec=grid_spec, out_shape=jax.ShapeDtypeStruct((db, rows, kd), F32),
        compiler_params=_cparams(("arbitrary", "arbitrary")),
        name="moba_sample" if moba else "fox_sample")(
            page_table.reshape(-1), *([cache_k4] * pg), *([cache_v4] * pg), *ins)


def _wo_body(x_ref, mix_ref, w_ref, o_ref):
    o_ref[...] = x_ref[...] + jnp.dot(mix_ref[...], w_ref[...], preferred_element_type=F32)


def _wo_residual(x, mix_bf, wo_bf, layer, *, tm):
    rows, d = x.shape
    tm = _row_tile(rows, tm)
    return pl.pallas_call(
        _wo_body, grid=(rows // tm,),
        in_specs=[pl.BlockSpec((tm, d), lambda i: (i, 0)),
                  pl.BlockSpec((tm, mix_bf.shape[1]), lambda i: (i, 0)),
                  pl.BlockSpec((None,) + wo_bf.shape[1:], lambda i: (layer, 0, 0))],
        out_specs=pl.BlockSpec((tm, d), lambda i: (i, 0)),
        out_shape=jax.ShapeDtypeStruct((rows, d), F32),
        compiler_params=_cparams(("arbitrary",)), name="wo_residual")(x, mix_bf, wo_bf)


def _mlp_body(h_ref, g_ref, wu_ref, wd_ref, o_ref, hn_sc):
    f = pl.program_id(1)

    @pl.when(f == 0)
    def _():
        h = h_ref[...]
        ms = jnp.mean(h * h, axis=-1, keepdims=True)
        hn_sc[...] = (h * lax.rsqrt(ms + RMS_EPS) * g_ref[...]).astype(BF16)
        o_ref[...] = h

    u = jnp.maximum(jnp.dot(hn_sc[...], wu_ref[...], preferred_element_type=F32), 0.0)
    o_ref[...] += jnp.dot((u * u).astype(BF16), wd_ref[...], preferred_element_type=F32)


def _mlp_residual(h, g_mlp, w_up, w_down, layer, *, tm, tf):
    rows, d = h.shape
    dff = w_up.shape[2]
    tm = _row_tile(rows, tm)
    tf = min(tf, dff)
    assert dff % tf == 0
    return pl.pallas_call(
        _mlp_body, grid=(rows // tm, dff // tf),
        in_specs=[pl.BlockSpec((tm, d), lambda i, f: (i, 0)),
                  pl.BlockSpec(g_mlp.shape, lambda i, f: (0, 0)),
                  pl.BlockSpec((None, d, tf), lambda i, f: (layer, 0, f)),
                  pl.BlockSpec((None, tf, d), lambda i, f: (layer, f, 0))],
        out_specs=pl.BlockSpec((tm, d), lambda i, f: (i, 0)),
        out_shape=jax.ShapeDtypeStruct((rows, d), F32),
        scratch_shapes=[pltpu.VMEM((tm, d), BF16)],
        compiler_params=_cparams(("arbitrary", "arbitrary")),
        name="mlp_residual")(h, g_mlp, w_up, w_down)


def kernel(x_prompt, x_sample, cache_k, cache_v, cache_logf, page_table, attn_norm, w_qkv, q_norm, k_norm,
           w_fgate, b_fgate, w_o, mlp_norm, w_up, w_down):
    batch, seq, d = x_prompt.shape
    db, dec_seq, _ = x_sample.shape
    depth, n_pool, page, n_kv, hd = cache_k.shape
    n_heads = w_fgate.shape[2]
    group = n_heads // n_kv
    qd, kd = n_heads * hd, n_kv * hd
    n_pages = page_table.shape[1]
    past = n_pages * page
    rows_s = dec_seq * n_heads

    hp = x_prompt.reshape(batch * seq, d)
    hs = x_sample.reshape(db * dec_seq, d)
    cache_k4 = cache_k.reshape(depth, n_pool, page * n_kv, hd)
    cache_v4 = cache_v.reshape(depth, n_pool, page * n_kv, hd)
    row_kv = (jnp.arange(rows_s) % n_heads) // group
    row_mask = (row_kv[:, None] == jnp.arange(n_kv)[None, :]).astype(F32)
    slopes = jnp.exp2(-8.0 * jnp.arange(1, n_heads + 1, dtype=F32) / n_heads)

    def block_diag(q2d, dtype):
        q = q2d.astype(F32).reshape(db, rows_s, 1, hd) * row_mask[None, :, :, None]
        return q.reshape(db, rows_s, kd).astype(dtype)

    def pad_page(a2d):
        a = a2d.reshape(db, dec_seq, kd)
        return jnp.pad(a, ((0, 0), (0, LANE - dec_seq), (0, 0)))

    def own_lanes(o):
        o6 = o.reshape(db, dec_seq, n_kv, group, n_kv, hd)
        return jnp.stack([o6[:, :, k, :, k, :] for k in range(n_kv)], axis=2).reshape(db * dec_seq, qd)

    outs = {k: [] for k in ("kp", "vp", "fp", "ks", "vs", "fs")}
    w_qkv_bf, wo_bf = w_qkv.astype(BF16), w_o.astype(BF16)
    w_up_bf, w_down_bf = w_up.astype(BF16), w_down.astype(BF16)
    for i in range(depth):
        fox = i % 2 == 0
        j = i // 2
        g_attn, qg, kg = attn_norm[i][None, :], q_norm[i][None, :], k_norm[i][None, :]
        safe = _fixed_max_is_safe(qg, kg, hd)
        if fox:
            wf_bf = jnp.pad(w_fgate[j], ((0, 0), (0, LANE - n_heads))).astype(BF16)
            bf_pad = jnp.pad(b_fgate[j], (0, LANE - n_heads))[None, :]
        else:
            wf_bf = bf_pad = None
        proj = functools.partial(_qkv_proj, g_attn=g_attn, w_bf=w_qkv_bf, q_gain=qg, k_gain=kg, wf_bf=wf_bf,
                                 bf_pad=bf_pad, layer=i, n_heads=n_heads, n_kv=n_kv, hd=hd, q_f32=not fox)
        res_p = proj(hp, attn_copies=True, tm=512)
        res_s = proj(hs, attn_copies=False, tm=128)
        qp, kp32, vp32, kpb, vpt = res_p[:5]
        qs, ks32, vs32 = res_s[:3]
        outs["kp"].append(kp32.reshape(batch, seq, n_kv, hd))
        outs["vp"].append(vp32.reshape(batch, seq, n_kv, hd))
        outs["ks"].append(ks32.reshape(db, dec_seq, n_kv, hd))
        outs["vs"].append(vs32.reshape(db, dec_seq, n_kv, hd))
        kn_pad, vn_pad = pad_page(ks32), pad_page(vs32)
        if fox:
            lfp, lfp16, lfs16 = res_p[5], res_p[6], res_s[4]
            outs["fp"].append(lfp16.reshape(batch, seq, n_heads))
            outs["fs"].append(lfs16.reshape(db, dec_seq, n_heads))
            qx, kx = _fox_aug(lfp, batch=batch, seq=seq, n_heads=n_heads, n_kv=n_kv)
            mix_p = _flash_prompt(qp, qx, kpb, kx, vpt, None, safe, batch=batch, seq=seq, n_heads=n_heads,
                                  n_kv=n_kv, hd=hd, moba=False)
            logf_t = jnp.swapaxes(cache_logf[j], 1, 2)
            lfnew_t = jnp.pad(jnp.swapaxes(lfs16.reshape(db, dec_seq, n_heads), 1, 2),
                              ((0, 0), (0, 0), (0, LANE - dec_seq)))
            c_t, cnew_t = _cum_sample(page_table, logf_t, lfnew_t, n_heads=n_heads)
            cq = jnp.swapaxes(cnew_t[:, :, :dec_seq], 1, 2).reshape(db, rows_s, 1)
            o_s = _sample_attention(page_table, cache_k4, cache_v4, i, block_diag(qs, BF16), kn_pad, vn_pad,
                                    (cq, c_t, cnew_t), moba=False, dec_seq=dec_seq, n_heads=n_heads)
        else:
            kmean = _kmean(kp32, batch=batch, seq=seq, n_kv=n_kv)
            qx, kx = _moba_static_cols(seq, n_heads)
            mix_p = _flash_prompt(qp, qx, kpb, kx, vpt, kmean, safe, batch=batch, seq=seq, n_heads=n_heads,
                                  n_kv=n_kv, hd=hd, moba=True)
            slope_col = (jnp.tile(slopes, dec_seq) * LOG2E)[:, None]
            q_scaled = qs * (hd ** -0.5 * LOG2E)
            o_s = _sample_attention(page_table, cache_k4, cache_v4, i, block_diag(q_scaled, BF16), kn_pad,
                                    vn_pad, (block_diag(qs, F32), slope_col), moba=True, dec_seq=dec_seq,
                                    n_heads=n_heads)
        mix_s = own_lanes(o_s).astype(BF16)
        g_mlp = mlp_norm[i][None, :]
        hp = _wo_residual(hp, mix_p, wo_bf, i, tm=512)
        hs = _wo_residual(hs, mix_s, wo_bf, i, tm=128)
        hp = _mlp_residual(hp, g_mlp, w_up_bf, w_down_bf, i, tm=1024, tf=512)
        hs = _mlp_residual(hs, g_mlp, w_up_bf, w_down_bf, i, tm=128, tf=512)
    return (hp.reshape(batch, seq, d), hs.reshape(db, dec_seq, d),
            jnp.stack(outs["kp"]), jnp.stack(outs["vp"]), jnp.stack(outs["fp"]),
            jnp.stack(outs["ks"]), jnp.stack(outs["vs"]), jnp.stack(outs["fs"]))
```

```python
import functools
import math

import numpy as np
import jax
import jax.numpy as jnp
from jax import lax
from jax.experimental import pallas as pl
from jax.experimental.pallas import tpu as pltpu

F32 = jnp.float32
BF16 = jnp.bfloat16
LOG2E = 1.4426950408889634
RMS_EPS = 1e-6
MOBA_BLOCK = 256
MOBA_TOPK = 3
LANE = 128
ATTN_TILE = 512
FLASH_CHUNK = 512
DENOM_ROWS = 16
FIXED_MAX_HEADROOM = 64.0
PAGES_PER_STEP = 32
NEG_BIG = -(2.0 ** 100)
VMEM_LIMIT = 56 * 1024 * 1024
COL_BLK = 0
COL_QPOS = 64
COL_KPOS = 67


def _cparams(sem):
    return pltpu.CompilerParams(dimension_semantics=sem, vmem_limit_bytes=VMEM_LIMIT)


def _row_tile(rows, pref):
    t = min(rows, pref)
    while rows % t:
        t -= 8
    return t


def _split3(x):
    hi = x.astype(BF16).astype(F32)
    r1 = x - hi
    mid = r1.astype(BF16).astype(F32)
    lo = (r1 - mid).astype(BF16).astype(F32)
    return hi, mid, lo


def _dot_nt(a, b):
    return lax.dot_general(a, b, (((1,), (1,)), ((), ())), preferred_element_type=F32)


def _qkv_body(*refs, n_heads, n_kv, hd, q_scale, with_gate, q_f32, attn_copies):
    n_in = 7 if with_gate else 5
    x_ref, g_ref, w_ref, qg_ref, kg_ref = refs[:5]
    outs = list(refs[n_in:])
    q_ref, k32_ref, v32_ref = outs[:3]
    del outs[:3]
    if attn_copies:
        kb_ref, vt_ref = outs[:2]
        del outs[:2]
    qd, kd = n_heads * hd, n_kv * hd
    x = x_ref[...]
    ms = jnp.mean(x * x, axis=-1, keepdims=True)
    xn = (x * lax.rsqrt(ms + RMS_EPS) * g_ref[...]).astype(BF16)

    def head_norm(a, gain):
        m = jnp.mean(a * a, axis=-1, keepdims=True)
        return a * lax.rsqrt(m + RMS_EPS) * gain

    chunk = 4 * hd
    for c in range(qd // chunk):
        acc = jnp.dot(xn, w_ref[:, c * chunk:(c + 1) * chunk], preferred_element_type=F32)
        for j in range(4):
            qn = head_norm(acc[:, j * hd:(j + 1) * hd], qg_ref[...])
            lo = c * chunk + j * hd
            if q_f32:
                q_ref[:, lo:lo + hd] = qn
            else:
                q_ref[:, lo:lo + hd] = (qn * q_scale).astype(BF16)
    tm = x.shape[0]
    acc = jnp.dot(xn, w_ref[:, qd:qd + kd], preferred_element_type=F32)
    for j in range(n_kv):
        kn = head_norm(acc[:, j * hd:(j + 1) * hd], kg_ref[...])
        k32_ref[pl.ds(j, tm, stride=n_kv), :] = kn
        if attn_copies:
            kb_ref[:, j * hd:(j + 1) * hd] = kn.astype(BF16)
    acc = jnp.dot(xn, w_ref[:, qd + kd:qd + 2 * kd], preferred_element_type=F32)
    for j in range(n_kv):
        v32_ref[pl.ds(j, tm, stride=n_kv), :] = acc[:, j * hd:(j + 1) * hd]
    if attn_copies:
        for j in range(n_kv):
            vt_ref[j] = acc[:, j * hd:(j + 1) * hd].T.astype(BF16)
    if with_gate:
        wf_ref, bf_ref = refs[5:7]
        z = jnp.dot(xn, wf_ref[...], preferred_element_type=F32) + bf_ref[...]
        lf = jnp.minimum(z, 0.0) - jnp.log1p(jnp.exp(-jnp.abs(z)))
        outs[0][...] = lf
        outs[1][...] = lf[:, :n_heads]


def _qkv_proj(x, g_attn, w_bf, q_gain, k_gain, wf_bf, bf_pad, *, layer, n_heads, n_kv, hd, q_f32,
              attn_copies, tm):
    rows, d = x.shape
    qd, kd = n_heads * hd, n_kv * hd
    with_gate = wf_bf is not None
    tm = _row_tile(rows, tm)
    row_spec = lambda w: pl.BlockSpec((tm, w), lambda i: (i, 0))
    full = lambda a: pl.BlockSpec(a.shape, lambda i: (0,) * a.ndim)
    ins = [x, g_attn, w_bf, q_gain, k_gain]
    in_specs = [row_spec(d), full(g_attn), pl.BlockSpec((None,) + w_bf.shape[1:], lambda i: (layer, 0, 0)),
                full(q_gain), full(k_gain)]
    kv_spec = pl.BlockSpec((tm * n_kv, hd), lambda i: (i, 0))
    out_shape = [jax.ShapeDtypeStruct((rows, qd), F32 if q_f32 else BF16),
                 jax.ShapeDtypeStruct((rows * n_kv, hd), F32), jax.ShapeDtypeStruct((rows * n_kv, hd), F32)]
    out_specs = [row_spec(qd), kv_spec, kv_spec]
    if attn_copies:
        assert tm == ATTN_TILE
        out_shape += [jax.ShapeDtypeStruct((rows, kd), BF16),
                      jax.ShapeDtypeStruct((n_kv, rows // tm, hd, tm), BF16)]
        out_specs += [row_spec(kd), pl.BlockSpec((n_kv, None, hd, tm), lambda i: (0, i, 0, 0))]
    if with_gate:
        ins += [wf_bf, bf_pad]
        in_specs += [full(wf_bf), full(bf_pad)]
        out_shape += [jax.ShapeDtypeStruct((rows, LANE), F32), jax.ShapeDtypeStruct((rows, n_heads), F32)]
        out_specs += [row_spec(LANE), row_spec(n_heads)]
    body = functools.partial(_qkv_body, n_heads=n_heads, n_kv=n_kv, hd=hd,
                             q_scale=hd ** -0.5 * LOG2E, with_gate=with_gate, q_f32=q_f32,
                             attn_copies=attn_copies)
    return pl.pallas_call(body, grid=(rows // tm,), in_specs=in_specs, out_specs=out_specs,
                          out_shape=out_shape, compiler_params=_cparams(("arbitrary",)),
                          name="qkv_proj")(*ins)


def _fox_aug_body(lf_ref, qx_ref, kx_ref, carry_ref, *, n_heads, n_kv):
    t = pl.program_id(1)

    @pl.when(t == 0)
    def _():
        carry_ref[...] = jnp.zeros_like(carry_ref)

    lf = lf_ref[...]
    tc = lf.shape[0]
    tri = (lax.broadcasted_iota(jnp.int32, (tc, tc), 1)
           <= lax.broadcasted_iota(jnp.int32, (tc, tc), 0)).astype(F32)
    c = jnp.dot(tri, lf, preferred_element_type=F32, precision=lax.Precision.HIGHEST) + carry_ref[...]
    carry_ref[...] = c[tc - 1:tc, :]
    c2 = c * LOG2E
    lane = lax.broadcasted_iota(jnp.int32, (tc, LANE), 1)
    group = n_heads // n_kv
    ones3 = jnp.where(lane < 3, 1.0, 0.0)
    for kvh in range(n_kv):
        kaug = ones3
        for g in range(group):
            h = kvh * group + g
            hi, mid, lo = _split3(jnp.broadcast_to(c2[:, h:h + 1], (tc, LANE)))
            base = 3 + 3 * g
            own = jnp.where((lane >= base) & (lane < base + 3), 1.0, 0.0)
            qaug = jnp.where(lane == 0, hi, jnp.where(lane == 1, mid, jnp.where(lane == 2, lo, own)))
            qx_ref[:, h * LANE:(h + 1) * LANE] = qaug.astype(BF16)
            kaug = jnp.where(lane == base, -hi,
                             jnp.where(lane == base + 1, -mid, jnp.where(lane == base + 2, -lo, kaug)))
        kx_ref[:, kvh * LANE:(kvh + 1) * LANE] = kaug.astype(BF16)


def _fox_aug(lf128, *, batch, seq, n_heads, n_kv):
    tc = _row_tile(seq, 512)
    nt = seq // tc
    body = functools.partial(_fox_aug_body, n_heads=n_heads, n_kv=n_kv)
    return pl.pallas_call(
        body, grid=(batch, nt),
        in_specs=[pl.BlockSpec((tc, LANE), lambda b, t: (b * nt + t, 0))],
        out_specs=[pl.BlockSpec((tc, n_heads * LANE), lambda b, t: (b * nt + t, 0)),
                   pl.BlockSpec((tc, n_kv * LANE), lambda b, t: (b * nt + t, 0))],
        out_shape=[jax.ShapeDtypeStruct((batch * seq, n_heads * LANE), BF16),
                   jax.ShapeDtypeStruct((batch * seq, n_kv * LANE), BF16)],
        scratch_shapes=[pltpu.VMEM((1, LANE), F32)],
        compiler_params=_cparams(("arbitrary", "arbitrary")), name="fox_aug")(lf128)


def _kmean_body(k_ref, o_ref, *, nblk, n_kv):
    hd = k_ref.shape[1]
    for j in range(nblk):
        for h in range(n_kv):
            blk = k_ref[pl.ds(j * MOBA_BLOCK * n_kv + h, MOBA_BLOCK, stride=n_kv), :]
            o_ref[j:j + 1, h * hd:(h + 1) * hd] = jnp.sum(blk, axis=0, keepdims=True) * (1.0 / MOBA_BLOCK)
    if o_ref.shape[0] > nblk:
        o_ref[nblk:, :] = jnp.zeros((o_ref.shape[0] - nblk, o_ref.shape[1]), F32)


def _kmean(k32, *, batch, seq, n_kv):
    hd = k32.shape[1]
    nblk = seq // MOBA_BLOCK
    assert nblk <= COL_QPOS
    return pl.pallas_call(
        functools.partial(_kmean_body, nblk=nblk, n_kv=n_kv), grid=(batch,),
        in_specs=[pl.BlockSpec((seq * n_kv, hd), lambda b: (b, 0))],
        out_specs=pl.BlockSpec((None, LANE, n_kv * hd), lambda b: (b, 0, 0)),
        out_shape=jax.ShapeDtypeStruct((batch, LANE, n_kv * hd), F32),
        compiler_params=_cparams(("arbitrary",)), name="moba_kmean")(k32)


def _top3_allowed(gate, lane, own):
    lane_f = lane.astype(F32)
    gm = jnp.where(lane < own, gate, -jnp.inf)
    sel = jnp.zeros(gate.shape, F32)
    for _ in range(MOBA_TOPK):
        mx = jnp.max(gm, axis=1, keepdims=True)
        idx = jnp.min(jnp.where(gm == mx, lane_f, float(LANE)), axis=1, keepdims=True)
        pick = lane_f == idx
        sel = jnp.where(pick, 1.0, sel)
        gm = jnp.where(pick, -jnp.inf, gm)
    return jnp.where(lane < own, sel, jnp.where(lane == own, 1.0, 0.0))


def _fixed_max_is_safe(q_gain, k_gain, hd):
    b = math.sqrt(hd) * LOG2E * jnp.max(jnp.abs(q_gain)) * jnp.max(jnp.abs(k_gain))
    return (2.0 * 1.02 * b + 2.0 < FIXED_MAX_HEADROOM).astype(jnp.int32).reshape(1)


def _flash_body(safe_ref, *refs, group, hd, moba, tile, nblk):
    if moba:
        q_ref, qx_ref, k_ref, kx_ref, vt_ref, km_ref, o_ref, qaug, m_sc, acc_sc, qlo = refs
    else:
        q_ref, qx_ref, k_ref, kx_ref, vt_ref, o_ref, qaug, m_sc, acc_sc = refs
    i = pl.program_id(2)

    def moba_selection_bias():
        nbr = -(-nblk // 16) * 16
        km = km_ref[:nbr, :]
        km_hi = km.astype(BF16)
        km_lo = (km - km_hi.astype(F32)).astype(BF16)
        g1 = _dot_nt(jnp.concatenate([km_hi, km_lo], axis=0), qaug[:, :hd])
        gate = g1[:nbr, :] + g1[nbr:, :] + _dot_nt(km_hi, qlo[...])
        blk = lax.broadcasted_iota(jnp.int32, gate.shape, 0)
        pos = i * tile + (lax.broadcasted_iota(jnp.int32, gate.shape, 1) & (tile - 1))
        own = lax.shift_right_logical(pos, int(math.log2(MOBA_BLOCK)))
        blk_f = blk.astype(F32)
        gm = jnp.where(blk < own, gate, -jnp.inf)
        sel = jnp.zeros(gate.shape, F32)
        for _ in range(MOBA_TOPK):
            mx = jnp.max(gm, axis=0, keepdims=True)
            idx = jnp.min(jnp.where(gm == mx, blk_f, float(LANE)), axis=0, keepdims=True)
            pick = blk_f == idx
            sel = jnp.where(pick, 1.0, sel)
            gm = jnp.where(pick, -jnp.inf, gm)
        allowed = jnp.where(blk < own, sel, jnp.where(blk == own, 1.0, 0.0))
        bias = jnp.where(allowed > 0.0, 0.0, NEG_BIG)
        return jnp.concatenate([bias, jnp.zeros((LANE - nbr, gate.shape[1]), F32)], axis=0)

    for g in range(group):
        rows = slice(g * tile, (g + 1) * tile)
        if moba:
            qs = q_ref[:, g * hd:(g + 1) * hd] * (hd ** -0.5 * LOG2E)
            hi = qs.astype(BF16)
            qaug[rows, :hd] = hi
            qlo[rows, :] = (qs - hi.astype(F32)).astype(BF16)
        else:
            qaug[rows, :hd] = q_ref[:, g * hd:(g + 1) * hd]
            qaug[rows, hd:] = qx_ref[:, g * LANE:(g + 1) * LANE]
    if moba:
        bias_t = moba_selection_bias()
        lane = lax.broadcasted_iota(jnp.int32, (tile, LANE), 1)
        for g in range(group):
            aug = jnp.where(lane < COL_QPOS, bias_t[:, g * tile:(g + 1) * tile].T,
                            qx_ref[:, g * LANE:(g + 1) * LANE].astype(F32))
            qaug[g * tile:(g + 1) * tile, hd:] = aug.astype(BF16)
    m_sc[...] = jnp.full(m_sc.shape, -jnp.inf, F32)
    acc_sc[...] = jnp.zeros(acc_sc.shape, F32)

    def step(j, diagonal, fixed_max=False):
        keys = pl.ds(pl.multiple_of(j * tile, tile), tile)
        kaug = jnp.concatenate([k_ref[keys, :], kx_ref[keys, :]], axis=1)
        vt = jnp.concatenate([vt_ref[j], jnp.ones((DENOM_ROWS, tile), BF16)], axis=0)
        n_chunks = group * tile // FLASH_CHUNK
        scores = [_dot_nt(kaug, qaug[0:FLASH_CHUNK, :])]
        for c in range(n_chunks):
            cols = slice(c * FLASH_CHUNK, (c + 1) * FLASH_CHUNK)
            if c + 1 < n_chunks:
                scores.append(_dot_nt(kaug, qaug[(c + 1) * FLASH_CHUNK:(c + 2) * FLASH_CHUNK, :]))
            st = scores[c]
            if fixed_max:
                pt = jnp.exp2(st - m_sc[:, cols])
                acc_sc[:, cols] += jnp.dot(vt, pt.astype(BF16), preferred_element_type=F32)
                continue
            if diagonal:
                key = lax.broadcasted_iota(jnp.int32, st.shape, 0)
                query = (c * FLASH_CHUNK) % tile + lax.broadcasted_iota(jnp.int32, st.shape, 1)
                st = jnp.where(key <= query, st, -jnp.inf)
            m_prev = m_sc[:, cols]
            m_new = jnp.maximum(m_prev, jnp.max(st, axis=0, keepdims=True))
            alpha = jnp.exp2(m_prev - m_new)
            pt = jnp.exp2(st - m_new)
            acc_sc[:, cols] = alpha * acc_sc[:, cols] + jnp.dot(vt, pt.astype(BF16),
                                                                preferred_element_type=F32)
            m_sc[:, cols] = m_new

    step(i, True)
    fixed = safe_ref[0] == 1

    @pl.when(fixed)
    def _():
        lax.fori_loop(0, i, lambda j, carry: (step(j, False, fixed_max=True), carry)[1], 0)

    @pl.when(jnp.logical_not(fixed))
    def _():
        lax.fori_loop(0, i, lambda j, carry: (step(j, False), carry)[1], 0)

    out_t = acc_sc[:hd, :] / acc_sc[hd:hd + 1, :]
    for g in range(group):
        o_ref[:, g * hd:(g + 1) * hd] = out_t[:, g * tile:(g + 1) * tile].T.astype(o_ref.dtype)


def _flash_prompt(q, qx, kb, kx, vt, kmean, safe, *, batch, seq, n_heads, n_kv, hd, moba):
    tile = ATTN_TILE
    assert seq % tile == 0 and tile % MOBA_BLOCK == 0 and tile & (tile - 1) == 0
    nt = seq // tile
    group = n_heads // n_kv
    gw = group * hd
    q_map = lambda b, h, i, *_: (b * nt + i, h)
    seq_map = lambda b, h, i, *_: (b, h)
    vt_spec = pl.BlockSpec((None, nt, hd, tile), lambda b, h, i, *_: (h, b, 0, 0))
    in_specs = [pl.BlockSpec((tile, gw), q_map)]
    if moba:
        in_specs += [pl.BlockSpec((tile, group * LANE), lambda b, h, i, *_: (i, h)),
                     pl.BlockSpec((seq, hd), seq_map),
                     pl.BlockSpec((seq, LANE), lambda b, h, i, *_: (0, 0)),
                     vt_spec,
                     pl.BlockSpec((None, LANE, hd), lambda b, h, i, *_: (b, 0, h))]
        ins = [q, qx, kb, kx, vt, kmean]
    else:
        in_specs += [pl.BlockSpec((tile, group * LANE), q_map),
                     pl.BlockSpec((seq, hd), seq_map),
                     pl.BlockSpec((seq, LANE), seq_map),
                     vt_spec]
        ins = [q, qx, kb, kx, vt]
    body = functools.partial(_flash_body, group=group, hd=hd, moba=moba, tile=tile,
                             nblk=seq // MOBA_BLOCK)
    scratch = [pltpu.VMEM((group * tile, hd + LANE), BF16), pltpu.VMEM((1, group * tile), F32),
               pltpu.VMEM((hd + DENOM_ROWS, group * tile), F32)]
    if moba:
        scratch.append(pltpu.VMEM((group * tile, hd), BF16))
    grid_spec = pltpu.PrefetchScalarGridSpec(
        num_scalar_prefetch=1, grid=(batch, n_kv, nt), in_specs=in_specs,
        out_specs=pl.BlockSpec((tile, gw), q_map), scratch_shapes=scratch)
    return pl.pallas_call(
        body, grid_spec=grid_spec, out_shape=jax.ShapeDtypeStruct((batch * seq, n_heads * hd), BF16),
        compiler_params=_cparams(("arbitrary", "arbitrary", "arbitrary")),
        name="moba_prompt" if moba else "fox_prompt")(safe, *ins)


def _moba_static_cols(seq, n_heads):
    pos = jnp.arange(seq, dtype=jnp.int32)
    slopes = jnp.exp2(-8.0 * jnp.arange(1, n_heads + 1, dtype=F32) / n_heads) * LOG2E
    qparts = _split3(-(slopes[None, :] * pos.astype(F32)[:, None]))
    sparts = _split3(slopes)
    qx = jnp.zeros((seq, n_heads, LANE), F32)
    for c in range(3):
        qx = qx.at[:, :, COL_QPOS + c].set(qparts[c])
        qx = qx.at[:, :, COL_KPOS + c].set(jnp.broadcast_to(sparts[c][None, :], (seq, n_heads)))
        qx = qx.at[:, :, COL_KPOS + 3 + c].set(jnp.broadcast_to(sparts[c][None, :], (seq, n_heads)))
    blk = pos // MOBA_BLOCK
    lane = jnp.arange(LANE, dtype=jnp.int32)[None, :]
    kx = jnp.where(lane == blk[:, None], 1.0, 0.0)
    kx = jnp.where((lane >= COL_QPOS) & (lane < COL_QPOS + 3), 1.0, kx)
    kx = jnp.where((lane >= COL_KPOS) & (lane < COL_KPOS + 3), (blk * MOBA_BLOCK).astype(F32)[:, None], kx)
    kx = jnp.where((lane >= COL_KPOS + 3) & (lane < COL_KPOS + 6), (pos % MOBA_BLOCK).astype(F32)[:, None], kx)
    return qx.reshape(seq, n_heads * LANE).astype(BF16), kx.astype(BF16)


def _page_specs(n, block, layer_arg, n_pages, pages_per_step):
    def spec(i):
        def imap(b, s, pt):
            return (layer_arg, pt[b * n_pages + s * pages_per_step + i]) + (0,) * (len(block) - 2)
        return pl.BlockSpec(block, imap)
    return [spec(i) for i in range(n)]


def _cum_sample_body(pt_ref, *refs, pg):
    lf_refs, new_ref, c_ref, cnew_ref, carry_ref = refs[:pg], refs[pg], refs[pg + 1], refs[pg + 2], refs[pg + 3]
    s = pl.program_id(1)

    @pl.when(s == 0)
    def _():
        carry_ref[...] = jnp.zeros_like(carry_ref)

    triu = (lax.broadcasted_iota(jnp.int32, (LANE, LANE), 0)
            <= lax.broadcasted_iota(jnp.int32, (LANE, LANE), 1)).astype(F32)
    carry = carry_ref[...]
    nh = carry.shape[0]
    local = jnp.dot(jnp.concatenate([r[...] for r in lf_refs], axis=0), triu,
                    preferred_element_type=F32, precision=lax.Precision.HIGHEST)
    for i in range(pg):
        loc = local[i * nh:(i + 1) * nh, :]
        c_ref[:, i * LANE:(i + 1) * LANE] = loc + carry
        carry = carry + jnp.broadcast_to(loc[:, LANE - 1:LANE], loc.shape)
    carry_ref[...] = carry

    @pl.when(s == pl.num_programs(1) - 1)
    def _():
        cnew_ref[...] = jnp.dot(new_ref[...], triu, preferred_element_type=F32,
                                precision=lax.Precision.HIGHEST) + carry


def _cum_sample(page_table, logf_t, lfnew_t, *, n_heads):
    db, n_pages = page_table.shape
    page = logf_t.shape[2]
    assert page == LANE
    pg = math.gcd(PAGES_PER_STEP, n_pages)
    in_specs = _page_specs(pg, (None, None, n_heads, page), 0, n_pages, pg)
    in_specs.append(pl.BlockSpec((None, n_heads, LANE), lambda b, s, pt: (b, 0, 0)))
    grid_spec = pltpu.PrefetchScalarGridSpec(
        num_scalar_prefetch=1, grid=(db, n_pages // pg), in_specs=in_specs,
        out_specs=[pl.BlockSpec((None, n_heads, pg * page), lambda b, s, pt: (b, 0, s)),
                   pl.BlockSpec((None, n_heads, LANE), lambda b, s, pt: (b, 0, 0))],
        scratch_shapes=[pltpu.VMEM((n_heads, LANE), F32)])
    return pl.pallas_call(
        functools.partial(_cum_sample_body, pg=pg), grid_spec=grid_spec,
        out_shape=[jax.ShapeDtypeStruct((db, n_heads, n_pages * page), F32),
                   jax.ShapeDtypeStruct((db, n_heads, LANE), F32)],
        compiler_params=_cparams(("arbitrary", "arbitrary")),
        name="fox_sample_cumsum")(page_table.reshape(-1), *([logf_t[None]] * pg), lfnew_t)


def _row_token(shape, n_heads, dec_seq):
    row = lax.broadcasted_iota(jnp.int32, shape, 0)
    tok = jnp.zeros(shape, jnp.int32)
    for t in range(1, dec_seq):
        tok = tok + (row >= t * n_heads).astype(jnp.int32)
    return tok


def _stage_pages(k_refs, v_refs, kb, vb, page, n_kv, want_ksum=False):
    hd = kb.shape[1] // n_kv
    ksums = []
    for i, (k_ref, v_ref) in enumerate(zip(k_refs, v_refs)):
        parts = []
        for h in range(n_kv):
            kf = k_ref[pl.ds(h, page, stride=n_kv), :]
            kb[i * page:(i + 1) * page, h * hd:(h + 1) * hd] = kf.astype(BF16)
            vb[i * page:(i + 1) * page, h * hd:(h + 1) * hd] = (
                v_ref[pl.ds(h, page, stride=n_kv), :].astype(BF16))
            if want_ksum:
                parts.append(jnp.sum(kf, axis=0, keepdims=True))
        if want_ksum:
            ksums.append(jnp.concatenate(parts, axis=1))
    return ksums


def _fox_sample_body(pt_ref, *refs, pg, page, dec_seq, n_heads, n_kv):
    k_refs, v_refs = refs[:pg], refs[pg:2 * pg]
    (q_ref, cq_ref, c_ref, cnew_ref, kn_ref, vn_ref, o_ref, kb, vb, m_sc, l_sc, acc_sc) = refs[2 * pg:]
    s = pl.program_id(1)

    @pl.when(s == 0)
    def _():
        m_sc[...] = jnp.full(m_sc.shape, -jnp.inf, F32)
        l_sc[...] = jnp.zeros(l_sc.shape, F32)
        acc_sc[...] = jnp.zeros(acc_sc.shape, F32)

    def update(logits, v):
        m_prev = m_sc[...]
        m_new = jnp.maximum(m_prev, jnp.max(logits, axis=1, keepdims=True))
        alpha = jnp.exp2(m_prev - m_new)
        pr = jnp.exp2(logits - m_new)
        l_sc[...] = alpha * l_sc[...] + jnp.sum(pr, axis=1, keepdims=True)
        acc_sc[...] = alpha * acc_sc[...] + jnp.dot(pr.astype(BF16), v, preferred_element_type=F32)
        m_sc[...] = m_new

    q = q_ref[...]
    cq = cq_ref[...]
    _stage_pages(k_refs, v_refs, kb, vb, page, n_kv)
    ck = jnp.concatenate([c_ref[...]] * dec_seq, axis=0)
    update(_dot_nt(q, kb[...]) + (cq - ck) * LOG2E, vb[...])

    @pl.when(s == pl.num_programs(1) - 1)
    def _():
        rows = q.shape[0]
        ckn = jnp.concatenate([cnew_ref[...]] * dec_seq, axis=0)
        logits = _dot_nt(q, kn_ref[...].astype(BF16)) + (cq - ckn) * LOG2E
        tok = _row_token((rows, LANE), n_heads, dec_seq)
        logits = jnp.where(lax.broadcasted_iota(jnp.int32, (rows, LANE), 1) <= tok, logits, -jnp.inf)
        update(logits, vn_ref[...].astype(BF16))
        o_ref[...] = acc_sc[...] / l_sc[...]


def _moba_sample_body(pt_ref, *refs, pg, page, dec_seq, n_heads, n_kv, past):
    k_refs, v_refs = refs[:pg], refs[pg:2 * pg]
    (q_ref, q32_ref, slope_ref, kn_ref, vn_ref, o_ref,
     kb, vb, acc_all, m_all, l_all, km) = refs[2 * pg:]
    s = pl.program_id(1)
    nblk = past // MOBA_BLOCK
    ppb = MOBA_BLOCK // page
    rows = q_ref.shape[0]
    lane = lax.broadcasted_iota(jnp.int32, (rows, LANE), 1)
    qpos = (past + _row_token((rows, 1), n_heads, dec_seq)).astype(F32)
    km_row = lax.broadcasted_iota(jnp.int32, km.shape, 0)

    @pl.when(s == 0)
    def _():
        m_all[...] = jnp.full(m_all.shape, -jnp.inf, F32)
        l_all[...] = jnp.zeros(l_all.shape, F32)
        km[...] = jnp.zeros(km.shape, F32)

    def block_partial(logits, v, blk):
        mj = jnp.max(logits, axis=1, keepdims=True)
        pr = jnp.exp2(logits - mj)
        acc_all[blk] = jnp.dot(pr.astype(BF16), v, preferred_element_type=F32)
        m_all[...] = jnp.where(lane == blk, mj, m_all[...])
        l_all[...] = jnp.where(lane == blk, jnp.sum(pr, axis=1, keepdims=True), l_all[...])

    q = q_ref[...]
    slope = slope_ref[...]
    ksums = _stage_pages(k_refs, v_refs, kb, vb, page, n_kv, want_ksum=True)
    width = pg * page
    kpos = (s * width + lax.broadcasted_iota(jnp.int32, (1, width), 1)).astype(F32)
    logits = _dot_nt(q, kb[...]) - slope * (qpos - kpos)
    for jb in range(pg // ppb):
        blk = s * (pg // ppb) + jb
        block_partial(logits[:, jb * MOBA_BLOCK:(jb + 1) * MOBA_BLOCK],
                      vb[jb * MOBA_BLOCK:(jb + 1) * MOBA_BLOCK, :], blk)
        ksum = ksums[jb * ppb]
        for i in range(1, ppb):
            ksum = ksum + ksums[jb * ppb + i]
        km[...] = jnp.where(km_row == blk, ksum * (1.0 / MOBA_BLOCK), km[...])

    @pl.when(s == pl.num_programs(1) - 1)
    def _():
        npos = (past + lax.broadcasted_iota(jnp.int32, (1, LANE), 1)).astype(F32)
        ln = _dot_nt(q, kn_ref[...].astype(BF16)) - slope * (qpos - npos)
        ln = jnp.where(npos <= qpos, ln, -jnp.inf)
        block_partial(ln, vn_ref[...].astype(BF16), nblk)
        gate = lax.dot_general(q32_ref[...], km[...], (((1,), (1,)), ((), ())),
                               preferred_element_type=F32, precision=lax.Precision.HIGHEST)
        allowed = _top3_allowed(gate, lane, jnp.full((rows, LANE), nblk, jnp.int32)) > 0.0
        m_row = jnp.max(jnp.where(allowed, m_all[...], -jnp.inf), axis=1, keepdims=True)
        w = jnp.where(allowed, jnp.exp2(m_all[...] - m_row), 0.0)
        denom = jnp.sum(w * l_all[...], axis=1, keepdims=True)
        out = jnp.zeros(o_ref.shape, F32)
        for jb in range(nblk + 1):
            out = out + w[:, jb:jb + 1] * acc_all[jb]
        o_ref[...] = out / denom


def _sample_attention(page_table, cache_k4, cache_v4, layer, qbd, kn_pad, vn_pad, extra, *, moba,
                      dec_seq, n_heads):
    db, n_pages = page_table.shape
    kd = qbd.shape[2]
    hd = cache_k4.shape[3]
    n_kv = kd // hd
    page = cache_k4.shape[2] // n_kv
    rows = qbd.shape[1]
    past = n_pages * page
    pg = math.gcd(PAGES_PER_STEP, n_pages)
    assert page == LANE and MOBA_BLOCK % page == 0 and pg % (MOBA_BLOCK // page) == 0
    assert past % MOBA_BLOCK == 0 and past // MOBA_BLOCK < LANE
    page_specs = (_page_specs(pg, (None, None, page * n_kv, hd), layer, n_pages, pg)
                  + _page_specs(pg, (None, None, page * n_kv, hd), layer, n_pages, pg))
    per_b = lambda a: pl.BlockSpec((None,) + a.shape[1:], lambda b, s, pt: (b,) + (0,) * (a.ndim - 1))
    common = [pltpu.VMEM((pg * page, kd), BF16), pltpu.VMEM((pg * page, kd), BF16)]
    if moba:
        q32, slope = extra
        ins = [qbd, q32, slope, kn_pad, vn_pad]
        in_specs = page_specs + [per_b(qbd), per_b(q32),
                                 pl.BlockSpec(slope.shape, lambda b, s, pt: (0, 0)),
                                 per_b(kn_pad), per_b(vn_pad)]
        scratch = common + [pltpu.VMEM((past // MOBA_BLOCK + 1, rows, kd), F32),
                            pltpu.VMEM((rows, LANE), F32), pltpu.VMEM((rows, LANE), F32),
                            pltpu.VMEM((LANE, kd), F32)]
        body = functools.partial(_moba_sample_body, pg=pg, page=page, dec_seq=dec_seq,
                                 n_heads=n_heads, n_kv=n_kv, past=past)
    else:
        cq, c_t, cnew_t = extra
        ins = [qbd, cq, c_t, cnew_t, kn_pad, vn_pad]
        in_specs = page_specs + [per_b(qbd), per_b(cq),
                                 pl.BlockSpec((None, n_heads, pg * page), lambda b, s, pt: (b, 0, s)),
                                 per_b(cnew_t), per_b(kn_pad), per_b(vn_pad)]
        scratch = common + [pltpu.VMEM((rows, 1), F32), pltpu.VMEM((rows, 1), F32),
                            pltpu.VMEM((rows, kd), F32)]
        body = functools.partial(_fox_sample_body, pg=pg, page=page, dec_seq=dec_seq, n_heads=n_heads,
                                 n_kv=n_kv)
    grid_spec = pltpu.PrefetchScalarGridSpec(
        num_scalar_prefetch=1, grid=(db, n_pages // pg), in_specs=in_specs,
        out_specs=pl.BlockSpec((None, rows, kd), lambda b, s, pt: (b, 0, 0)), scratch_shapes=scratch)
    return pl.pallas_call(
        body, grid_spec=grid_spec, out_shape=jax.ShapeDtypeStruct((db, rows, kd), F32),
        compiler_params=_cparams(("arbitrary", "arbitrary")),
        name="moba_sample" if moba else "fox_sample")(
            page_table.reshape(-1), *([cache_k4] * pg), *([cache_v4] * pg), *ins)


def _wo_body(x_ref, mix_ref, w_ref, o_ref):
    o_ref[...] = x_ref[...] + jnp.dot(mix_ref[...], w_ref[...], preferred_element_type=F32)


def _wo_residual(x, mix_bf, wo_bf, layer, *, tm):
    rows, d = x.shape
    tm = _row_tile(rows, tm)
    return pl.pallas_call(
        _wo_body, grid=(rows // tm,),
        in_specs=[pl.BlockSpec((tm, d), lambda i: (i, 0)),
                  pl.BlockSpec((tm, mix_bf.shape[1]), lambda i: (i, 0)),
                  pl.BlockSpec((None,) + wo_bf.shape[1:], lambda i: (layer, 0, 0))],
        out_specs=pl.BlockSpec((tm, d), lambda i: (i, 0)),
        out_shape=jax.ShapeDtypeStruct((rows, d), F32),
        compiler_params=_cparams(("arbitrary",)), name="wo_residual")(x, mix_bf, wo_bf)


def _mlp_body(h_ref, g_ref, wu_ref, wd_ref, o_ref, hn_sc):
    f = pl.program_id(1)

    @pl.when(f == 0)
    def _():
        h = h_ref[...]
        ms = jnp.mean(h * h, axis=-1, keepdims=True)
        hn_sc[...] = (h * lax.rsqrt(ms + RMS_EPS) * g_ref[...]).astype(BF16)
        o_ref[...] = h

    u = jnp.maximum(jnp.dot(hn_sc[...], wu_ref[...].astype(BF16), preferred_element_type=F32), 0.0)
    o_ref[...] += jnp.dot((u * u).astype(BF16), wd_ref[...].astype(BF16), preferred_element_type=F32)


def _mlp_residual(h, g_mlp, w_up, w_down, layer, *, tm, tf):
    rows, d = h.shape
    dff = w_up.shape[2]
    tm = _row_tile(rows, tm)
    tf = min(tf, dff)
    assert dff % tf == 0
    return pl.pallas_call(
        _mlp_body, grid=(rows // tm, dff // tf),
        in_specs=[pl.BlockSpec((tm, d), lambda i, f: (i, 0)),
                  pl.BlockSpec(g_mlp.shape, lambda i, f: (0, 0)),
                  pl.BlockSpec((None, d, tf), lambda i, f: (layer, 0, f)),
                  pl.BlockSpec((None, tf, d), lambda i, f: (layer, f, 0))],
        out_specs=pl.BlockSpec((tm, d), lambda i, f: (i, 0)),
        out_shape=jax.ShapeDtypeStruct((rows, d), F32),
        scratch_shapes=[pltpu.VMEM((tm, d), BF16)],
        compiler_params=_cparams(("arbitrary", "arbitrary")),
        name="mlp_residual")(h, g_mlp, w_up, w_down)


def kernel(x_prompt, x_sample, cache_k, cache_v, cache_logf, page_table, attn_norm, w_qkv, q_norm, k_norm,
           w_fgate, b_fgate, w_o, mlp_norm, w_up, w_down):
    batch, seq, d = x_prompt.shape
    db, dec_seq, _ = x_sample.shape
    depth, n_pool, page, n_kv, hd = cache_k.shape
    n_heads = w_fgate.shape[2]
    group = n_heads // n_kv
    qd, kd = n_heads * hd, n_kv * hd
    n_pages = page_table.shape[1]
    past = n_pages * page
    rows_s = dec_seq * n_heads

    hp = x_prompt.reshape(batch * seq, d)
    hs = x_sample.reshape(db * dec_seq, d)
    cache_k4 = cache_k.reshape(depth, n_pool, page * n_kv, hd)
    cache_v4 = cache_v.reshape(depth, n_pool, page * n_kv, hd)
    row_kv = (jnp.arange(rows_s) % n_heads) // group
    row_mask = (row_kv[:, None] == jnp.arange(n_kv)[None, :]).astype(F32)
    slopes = jnp.exp2(-8.0 * jnp.arange(1, n_heads + 1, dtype=F32) / n_heads)

    def block_diag(q2d, dtype):
        q = q2d.astype(F32).reshape(db, rows_s, 1, hd) * row_mask[None, :, :, None]
        return q.reshape(db, rows_s, kd).astype(dtype)

    def pad_page(a2d):
        a = a2d.reshape(db, dec_seq, kd)
        return jnp.pad(a, ((0, 0), (0, LANE - dec_seq), (0, 0)))

    def own_lanes(o):
        o6 = o.reshape(db, dec_seq, n_kv, group, n_kv, hd)
        return jnp.stack([o6[:, :, k, :, k, :] for k in range(n_kv)], axis=2).reshape(db * dec_seq, qd)

    outs = {k: [] for k in ("kp", "vp", "fp", "ks", "vs", "fs")}
    w_qkv_bf, wo_bf = w_qkv.astype(BF16), w_o.astype(BF16)
    for i in range(depth):
        fox = i % 2 == 0
        j = i // 2
        g_attn, qg, kg = attn_norm[i][None, :], q_norm[i][None, :], k_norm[i][None, :]
        safe = _fixed_max_is_safe(qg, kg, hd)
        if fox:
            wf_bf = jnp.pad(w_fgate[j], ((0, 0), (0, LANE - n_heads))).astype(BF16)
            bf_pad = jnp.pad(b_fgate[j], (0, LANE - n_heads))[None, :]
        else:
            wf_bf = bf_pad = None
        proj = functools.partial(_qkv_proj, g_attn=g_attn, w_bf=w_qkv_bf, q_gain=qg, k_gain=kg, wf_bf=wf_bf,
                                 bf_pad=bf_pad, layer=i, n_heads=n_heads, n_kv=n_kv, hd=hd, q_f32=not fox)
        res_p = proj(hp, attn_copies=True, tm=512)
        res_s = proj(hs, attn_copies=False, tm=128)
        qp, kp32, vp32, kpb, vpt = res_p[:5]
        qs, ks32, vs32 = res_s[:3]
        outs["kp"].append(kp32.reshape(batch, seq, n_kv, hd))
        outs["vp"].append(vp32.reshape(batch, seq, n_kv, hd))
        outs["ks"].append(ks32.reshape(db, dec_seq, n_kv, hd))
        outs["vs"].append(vs32.reshape(db, dec_seq, n_kv, hd))
        kn_pad, vn_pad = pad_page(ks32), pad_page(vs32)
        if fox:
            lfp, lfp16, lfs16 = res_p[5], res_p[6], res_s[4]
            outs["fp"].append(lfp16.reshape(batch, seq, n_heads))
            outs["fs"].append(lfs16.reshape(db, dec_seq, n_heads))
            qx, kx = _fox_aug(lfp, batch=batch, seq=seq, n_heads=n_heads, n_kv=n_kv)
            mix_p = _flash_prompt(qp, qx, kpb, kx, vpt, None, safe, batch=batch, seq=seq, n_heads=n_heads,
                                  n_kv=n_kv, hd=hd, moba=False)
            logf_t = jnp.swapaxes(cache_logf[j], 1, 2)
            lfnew_t = jnp.pad(jnp.swapaxes(lfs16.reshape(db, dec_seq, n_heads), 1, 2),
                              ((0, 0), (0, 0), (0, LANE - dec_seq)))
            c_t, cnew_t = _cum_sample(page_table, logf_t, lfnew_t, n_heads=n_heads)
            cq = jnp.swapaxes(cnew_t[:, :, :dec_seq], 1, 2).reshape(db, rows_s, 1)
            o_s = _sample_attention(page_table, cache_k4, cache_v4, i, block_diag(qs, BF16), kn_pad, vn_pad,
                                    (cq, c_t, cnew_t), moba=False, dec_seq=dec_seq, n_heads=n_heads)
        else:
            kmean = _kmean(kp32, batch=batch, seq=seq, n_kv=n_kv)
            qx, kx = _moba_static_cols(seq, n_heads)
            mix_p = _flash_prompt(qp, qx, kpb, kx, vpt, kmean, safe, batch=batch, seq=seq, n_heads=n_heads,
                                  n_kv=n_kv, hd=hd, moba=True)
            slope_col = (jnp.tile(slopes, dec_seq) * LOG2E)[:, None]
            q_scaled = qs * (hd ** -0.5 * LOG2E)
            o_s = _sample_attention(page_table, cache_k4, cache_v4, i, block_diag(q_scaled, BF16), kn_pad,
                                    vn_pad, (block_diag(qs, F32), slope_col), moba=True, dec_seq=dec_seq,
                                    n_heads=n_heads)
        mix_s = own_lanes(o_s).astype(BF16)
        g_mlp = mlp_norm[i][None, :]
        hp = _wo_residual(hp, mix_p, wo_bf, i, tm=512)
        hs = _wo_residual(hs, mix_s, wo_bf, i, tm=128)
        hp = _mlp_residual(hp, g_mlp, w_up, w_down, i, tm=1024, tf=512)
        hs = _mlp_residual(hs, g_mlp, w_up, w_down, i, tm=128, tf=512)
    return (hp.reshape(batch, seq, d), hs.reshape(db, dec_seq, d),
            jnp.stack(outs["kp"]), jnp.stack(outs["vp"]), jnp.stack(outs["fp"]),
            jnp.stack(outs["ks"]), jnp.stack(outs["vs"]), jnp.stack(outs["fs"]))
```

```python
import functools
import math

import numpy as np
import jax
import jax.numpy as jnp
from jax import lax
from jax.experimental import pallas as pl
from jax.experimental.pallas import tpu as pltpu

F32 = jnp.float32
BF16 = jnp.bfloat16
LOG2E = 1.4426950408889634
RMS_EPS = 1e-6
MOBA_BLOCK = 256
MOBA_TOPK = 3
LANE = 128
ATTN_TILE = 512
FLASH_CHUNK = 512
DENOM_ROWS = 16
FIXED_MAX_HEADROOM = 64.0
PAGES_PER_STEP = 32
CUMSUM_PAGES_PER_STEP = 64
NEG_BIG = -(2.0 ** 100)
VMEM_LIMIT = 56 * 1024 * 1024
COL_BLK = 0
COL_QPOS = 64
COL_KPOS = 67


def _cparams(sem):
    return pltpu.CompilerParams(dimension_semantics=sem, vmem_limit_bytes=VMEM_LIMIT)


def _row_tile(rows, pref):
    t = min(rows, pref)
    while rows % t:
        t -= 8
    return t


def _split3(x):
    hi = x.astype(BF16).astype(F32)
    r1 = x - hi
    mid = r1.astype(BF16).astype(F32)
    lo = (r1 - mid).astype(BF16).astype(F32)
    return hi, mid, lo


def _dot_nt(a, b):
    return lax.dot_general(a, b, (((1,), (1,)), ((), ())), preferred_element_type=F32)


def _qkv_body(*refs, n_heads, n_kv, hd, q_scale, with_gate, q_f32, attn_copies):
    n_in = 7 if with_gate else 5
    x_ref, g_ref, w_ref, qg_ref, kg_ref = refs[:5]
    outs = list(refs[n_in:])
    q_ref, k32_ref, v32_ref = outs[:3]
    del outs[:3]
    if attn_copies:
        kb_ref, vt_ref = outs[:2]
        del outs[:2]
    qd, kd = n_heads * hd, n_kv * hd
    x = x_ref[...]
    ms = jnp.mean(x * x, axis=-1, keepdims=True)
    xn = (x * lax.rsqrt(ms + RMS_EPS) * g_ref[...]).astype(BF16)

    def head_norm(a, gain):
        m = jnp.mean(a * a, axis=-1, keepdims=True)
        return a * lax.rsqrt(m + RMS_EPS) * gain

    chunk = 4 * hd
    for c in range(qd // chunk):
        acc = jnp.dot(xn, w_ref[:, c * chunk:(c + 1) * chunk], preferred_element_type=F32)
        for j in range(4):
            qn = head_norm(acc[:, j * hd:(j + 1) * hd], qg_ref[...])
            lo = c * chunk + j * hd
            if q_f32:
                q_ref[:, lo:lo + hd] = qn
            else:
                q_ref[:, lo:lo + hd] = (qn * q_scale).astype(BF16)
    tm = x.shape[0]
    acc = jnp.dot(xn, w_ref[:, qd:qd + kd], preferred_element_type=F32)
    for j in range(n_kv):
        kn = head_norm(acc[:, j * hd:(j + 1) * hd], kg_ref[...])
        k32_ref[pl.ds(j, tm, stride=n_kv), :] = kn
        if attn_copies:
            kb_ref[:, j * hd:(j + 1) * hd] = kn.astype(BF16)
    acc = jnp.dot(xn, w_ref[:, qd + kd:qd + 2 * kd], preferred_element_type=F32)
    for j in range(n_kv):
        v32_ref[pl.ds(j, tm, stride=n_kv), :] = acc[:, j * hd:(j + 1) * hd]
    if attn_copies:
        for j in range(n_kv):
            vt_ref[j] = acc[:, j * hd:(j + 1) * hd].T.astype(BF16)
    if with_gate:
        wf_ref, bf_ref = refs[5:7]
        z = jnp.dot(xn, wf_ref[...], preferred_element_type=F32) + bf_ref[...]
        lf = jnp.minimum(z, 0.0) - jnp.log1p(jnp.exp(-jnp.abs(z)))
        outs[0][...] = lf
        outs[1][...] = lf[:, :n_heads]


def _qkv_proj(x, g_attn, w_bf, q_gain, k_gain, wf_bf, bf_pad, *, layer, n_heads, n_kv, hd, q_f32,
              attn_copies, tm):
    rows, d = x.shape
    qd, kd = n_heads * hd, n_kv * hd
    with_gate = wf_bf is not None
    tm = _row_tile(rows, tm)
    row_spec = lambda w: pl.BlockSpec((tm, w), lambda i: (i, 0))
    full = lambda a: pl.BlockSpec(a.shape, lambda i: (0,) * a.ndim)
    ins = [x, g_attn, w_bf, q_gain, k_gain]
    in_specs = [row_spec(d), full(g_attn), pl.BlockSpec((None,) + w_bf.shape[1:], lambda i: (layer, 0, 0)),
                full(q_gain), full(k_gain)]
    kv_spec = pl.BlockSpec((tm * n_kv, hd), lambda i: (i, 0))
    out_shape = [jax.ShapeDtypeStruct((rows, qd), F32 if q_f32 else BF16),
                 jax.ShapeDtypeStruct((rows * n_kv, hd), F32), jax.ShapeDtypeStruct((rows * n_kv, hd), F32)]
    out_specs = [row_spec(qd), kv_spec, kv_spec]
    if attn_copies:
        assert tm == ATTN_TILE
        out_shape += [jax.ShapeDtypeStruct((rows, kd), BF16),
                      jax.ShapeDtypeStruct((n_kv, rows // tm, hd, tm), BF16)]
        out_specs += [row_spec(kd), pl.BlockSpec((n_kv, None, hd, tm), lambda i: (0, i, 0, 0))]
    if with_gate:
        ins += [wf_bf, bf_pad]
        in_specs += [full(wf_bf), full(bf_pad)]
        out_shape += [jax.ShapeDtypeStruct((rows, LANE), F32), jax.ShapeDtypeStruct((rows, n_heads), F32)]
        out_specs += [row_spec(LANE), row_spec(n_heads)]
    body = functools.partial(_qkv_body, n_heads=n_heads, n_kv=n_kv, hd=hd,
                             q_scale=hd ** -0.5 * LOG2E, with_gate=with_gate, q_f32=q_f32,
                             attn_copies=attn_copies)
    return pl.pallas_call(body, grid=(rows // tm,), in_specs=in_specs, out_specs=out_specs,
                          out_shape=out_shape, compiler_params=_cparams(("arbitrary",)),
                          name="qkv_proj")(*ins)


def _fox_aug_body(lf_ref, qx_ref, kx_ref, carry_ref, *, n_heads, n_kv):
    t = pl.program_id(1)

    @pl.when(t == 0)
    def _():
        carry_ref[...] = jnp.zeros_like(carry_ref)

    lf = lf_ref[...]
    tc = lf.shape[0]
    tri = (lax.broadcasted_iota(jnp.int32, (tc, tc), 1)
           <= lax.broadcasted_iota(jnp.int32, (tc, tc), 0)).astype(F32)
    c = jnp.dot(tri, lf, preferred_element_type=F32, precision=lax.Precision.HIGHEST) + carry_ref[...]
    carry_ref[...] = c[tc - 1:tc, :]
    c2 = c * LOG2E
    lane = lax.broadcasted_iota(jnp.int32, (tc, LANE), 1)
    group = n_heads // n_kv
    ones3 = jnp.where(lane < 3, 1.0, 0.0)
    for kvh in range(n_kv):
        kaug = ones3
        for g in range(group):
            h = kvh * group + g
            hi, mid, lo = _split3(jnp.broadcast_to(c2[:, h:h + 1], (tc, LANE)))
            base = 3 + 3 * g
            own = jnp.where((lane >= base) & (lane < base + 3), 1.0, 0.0)
            qaug = jnp.where(lane == 0, hi, jnp.where(lane == 1, mid, jnp.where(lane == 2, lo, own)))
            qx_ref[:, h * LANE:(h + 1) * LANE] = qaug.astype(BF16)
            kaug = jnp.where(lane == base, -hi,
                             jnp.where(lane == base + 1, -mid, jnp.where(lane == base + 2, -lo, kaug)))
        kx_ref[:, kvh * LANE:(kvh + 1) * LANE] = kaug.astype(BF16)


def _fox_aug(lf128, *, batch, seq, n_heads, n_kv):
    tc = _row_tile(seq, 512)
    nt = seq // tc
    body = functools.partial(_fox_aug_body, n_heads=n_heads, n_kv=n_kv)
    return pl.pallas_call(
        body, grid=(batch, nt),
        in_specs=[pl.BlockSpec((tc, LANE), lambda b, t: (b * nt + t, 0))],
        out_specs=[pl.BlockSpec((tc, n_heads * LANE), lambda b, t: (b * nt + t, 0)),
                   pl.BlockSpec((tc, n_kv * LANE), lambda b, t: (b * nt + t, 0))],
        out_shape=[jax.ShapeDtypeStruct((batch * seq, n_heads * LANE), BF16),
                   jax.ShapeDtypeStruct((batch * seq, n_kv * LANE), BF16)],
        scratch_shapes=[pltpu.VMEM((1, LANE), F32)],
        compiler_params=_cparams(("arbitrary", "arbitrary")), name="fox_aug")(lf128)


def _kmean_body(k_ref, o_ref, *, nblk, n_kv):
    hd = k_ref.shape[1]
    for j in range(nblk):
        for h in range(n_kv):
            blk = k_ref[pl.ds(j * MOBA_BLOCK * n_kv + h, MOBA_BLOCK, stride=n_kv), :]
            o_ref[j:j + 1, h * hd:(h + 1) * hd] = jnp.sum(blk, axis=0, keepdims=True) * (1.0 / MOBA_BLOCK)
    if o_ref.shape[0] > nblk:
        o_ref[nblk:, :] = jnp.zeros((o_ref.shape[0] - nblk, o_ref.shape[1]), F32)


def _kmean(k32, *, batch, seq, n_kv):
    hd = k32.shape[1]
    nblk = seq // MOBA_BLOCK
    assert nblk <= COL_QPOS
    return pl.pallas_call(
        functools.partial(_kmean_body, nblk=nblk, n_kv=n_kv), grid=(batch,),
        in_specs=[pl.BlockSpec((seq * n_kv, hd), lambda b: (b, 0))],
        out_specs=pl.BlockSpec((None, LANE, n_kv * hd), lambda b: (b, 0, 0)),
        out_shape=jax.ShapeDtypeStruct((batch, LANE, n_kv * hd), F32),
        compiler_params=_cparams(("arbitrary",)), name="moba_kmean")(k32)


def _top3_allowed(gate, lane, own):
    lane_f = lane.astype(F32)
    gm = jnp.where(lane < own, gate, -jnp.inf)
    sel = jnp.zeros(gate.shape, F32)
    for _ in range(MOBA_TOPK):
        mx = jnp.max(gm, axis=1, keepdims=True)
        idx = jnp.min(jnp.where(gm == mx, lane_f, float(LANE)), axis=1, keepdims=True)
        pick = lane_f == idx
        sel = jnp.where(pick, 1.0, sel)
        gm = jnp.where(pick, -jnp.inf, gm)
    return jnp.where(lane < own, sel, jnp.where(lane == own, 1.0, 0.0))


def _fixed_max_is_safe(q_gain, k_gain, hd):
    b = math.sqrt(hd) * LOG2E * jnp.max(jnp.abs(q_gain)) * jnp.max(jnp.abs(k_gain))
    return (2.0 * 1.02 * b + 2.0 < FIXED_MAX_HEADROOM).astype(jnp.int32).reshape(1)


def _flash_body(safe_ref, *refs, group, hd, moba, tile, nblk):
    if moba:
        q_ref, qx_ref, k_ref, kx_ref, vt_ref, km_ref, o_ref, qaug, m_sc, acc_sc, qlo = refs
    else:
        q_ref, qx_ref, k_ref, kx_ref, vt_ref, o_ref, qaug, m_sc, acc_sc = refs
    i = pl.program_id(2)

    def moba_selection_bias():
        nbr = -(-nblk // 16) * 16
        km = km_ref[:nbr, :]
        km_hi = km.astype(BF16)
        km_lo = (km - km_hi.astype(F32)).astype(BF16)
        g1 = _dot_nt(jnp.concatenate([km_hi, km_lo], axis=0), qaug[:, :hd])
        gate = g1[:nbr, :] + g1[nbr:, :] + _dot_nt(km_hi, qlo[...])
        blk = lax.broadcasted_iota(jnp.int32, gate.shape, 0)
        pos = i * tile + (lax.broadcasted_iota(jnp.int32, gate.shape, 1) & (tile - 1))
        own = lax.shift_right_logical(pos, int(math.log2(MOBA_BLOCK)))
        blk_f = blk.astype(F32)
        gm = jnp.where(blk < own, gate, -jnp.inf)
        sel = jnp.zeros(gate.shape, F32)
        for _ in range(MOBA_TOPK):
            mx = jnp.max(gm, axis=0, keepdims=True)
            idx = jnp.min(jnp.where(gm == mx, blk_f, float(LANE)), axis=0, keepdims=True)
            pick = blk_f == idx
            sel = jnp.where(pick, 1.0, sel)
            gm = jnp.where(pick, -jnp.inf, gm)
        allowed = jnp.where(blk < own, sel, jnp.where(blk == own, 1.0, 0.0))
        bias = jnp.where(allowed > 0.0, 0.0, NEG_BIG)
        return jnp.concatenate([bias, jnp.zeros((LANE - nbr, gate.shape[1]), F32)], axis=0)

    for g in range(group):
        rows = slice(g * tile, (g + 1) * tile)
        if moba:
            qs = q_ref[:, g * hd:(g + 1) * hd] * (hd ** -0.5 * LOG2E)
            hi = qs.astype(BF16)
            qaug[rows, :hd] = hi
            qlo[rows, :] = (qs - hi.astype(F32)).astype(BF16)
        else:
            qaug[rows, :hd] = q_ref[:, g * hd:(g + 1) * hd]
            qaug[rows, hd:] = qx_ref[:, g * LANE:(g + 1) * LANE]
    if moba:
        bias_t = moba_selection_bias()
        lane = lax.broadcasted_iota(jnp.int32, (tile, LANE), 1)
        for g in range(group):
            aug = jnp.where(lane < COL_QPOS, bias_t[:, g * tile:(g + 1) * tile].T,
                            qx_ref[:, g * LANE:(g + 1) * LANE].astype(F32))
            qaug[g * tile:(g + 1) * tile, hd:] = aug.astype(BF16)
    m_sc[...] = jnp.full(m_sc.shape, -jnp.inf, F32)
    acc_sc[...] = jnp.zeros(acc_sc.shape, F32)

    def step(j, diagonal, fixed_max=False):
        keys = pl.ds(pl.multiple_of(j * tile, tile), tile)
        kaug = jnp.concatenate([k_ref[keys, :], kx_ref[keys, :]], axis=1)
        vt = jnp.concatenate([vt_ref[j], jnp.ones((DENOM_ROWS, tile), BF16)], axis=0)
        n_chunks = group * tile // FLASH_CHUNK
        scores = [_dot_nt(kaug, qaug[0:FLASH_CHUNK, :])]
        for c in range(n_chunks):
            cols = slice(c * FLASH_CHUNK, (c + 1) * FLASH_CHUNK)
            if c + 1 < n_chunks:
                scores.append(_dot_nt(kaug, qaug[(c + 1) * FLASH_CHUNK:(c + 2) * FLASH_CHUNK, :]))
            st = scores[c]
            if fixed_max:
                pt = jnp.exp2(st - m_sc[:, cols])
                acc_sc[:, cols] += jnp.dot(vt, pt.astype(BF16), preferred_element_type=F32)
                continue
            if diagonal:
                key = lax.broadcasted_iota(jnp.int32, st.shape, 0)
                query = (c * FLASH_CHUNK) % tile + lax.broadcasted_iota(jnp.int32, st.shape, 1)
                st = jnp.where(key <= query, st, -jnp.inf)
            m_prev = m_sc[:, cols]
            m_new = jnp.maximum(m_prev, jnp.max(st, axis=0, keepdims=True))
            alpha = jnp.exp2(m_prev - m_new)
            pt = jnp.exp2(st - m_new)
            acc_sc[:, cols] = alpha * acc_sc[:, cols] + jnp.dot(vt, pt.astype(BF16),
                                                                preferred_element_type=F32)
            m_sc[:, cols] = m_new

    step(i, True)
    fixed = safe_ref[0] == 1

    @pl.when(fixed)
    def _():
        lax.fori_loop(0, i, lambda j, carry: (step(j, False, fixed_max=True), carry)[1], 0)

    @pl.when(jnp.logical_not(fixed))
    def _():
        lax.fori_loop(0, i, lambda j, carry: (step(j, False), carry)[1], 0)

    out_t = acc_sc[:hd, :] / acc_sc[hd:hd + 1, :]
    for g in range(group):
        o_ref[:, g * hd:(g + 1) * hd] = out_t[:, g * tile:(g + 1) * tile].T.astype(o_ref.dtype)


def _flash_prompt(q, qx, kb, kx, vt, kmean, safe, *, batch, seq, n_heads, n_kv, hd, moba):
    tile = ATTN_TILE
    assert seq % tile == 0 and tile % MOBA_BLOCK == 0 and tile & (tile - 1) == 0
    nt = seq // tile
    group = n_heads // n_kv
    gw = group * hd
    q_map = lambda b, h, i, *_: (b * nt + i, h)
    seq_map = lambda b, h, i, *_: (b, h)
    vt_spec = pl.BlockSpec((None, nt, hd, tile), lambda b, h, i, *_: (h, b, 0, 0))
    in_specs = [pl.BlockSpec((tile, gw), q_map)]
    if moba:
        in_specs += [pl.BlockSpec((tile, group * LANE), lambda b, h, i, *_: (i, h)),
                     pl.BlockSpec((seq, hd), seq_map),
                     pl.BlockSpec((seq, LANE), lambda b, h, i, *_: (0, 0)),
                     vt_spec,
                     pl.BlockSpec((None, LANE, hd), lambda b, h, i, *_: (b, 0, h))]
        ins = [q, qx, kb, kx, vt, kmean]
    else:
        in_specs += [pl.BlockSpec((tile, group * LANE), q_map),
                     pl.BlockSpec((seq, hd), seq_map),
                     pl.BlockSpec((seq, LANE), seq_map),
                     vt_spec]
        ins = [q, qx, kb, kx, vt]
    body = functools.partial(_flash_body, group=group, hd=hd, moba=moba, tile=tile,
                             nblk=seq // MOBA_BLOCK)
    scratch = [pltpu.VMEM((group * tile, hd + LANE), BF16), pltpu.VMEM((1, group * tile), F32),
               pltpu.VMEM((hd + DENOM_ROWS, group * tile), F32)]
    if moba:
        scratch.append(pltpu.VMEM((group * tile, hd), BF16))
    grid_spec = pltpu.PrefetchScalarGridSpec(
        num_scalar_prefetch=1, grid=(batch, n_kv, nt), in_specs=in_specs,
        out_specs=pl.BlockSpec((tile, gw), q_map), scratch_shapes=scratch)
    return pl.pallas_call(
        body, grid_spec=grid_spec, out_shape=jax.ShapeDtypeStruct((batch * seq, n_heads * hd), BF16),
        compiler_params=_cparams(("arbitrary", "arbitrary", "arbitrary")),
        name="moba_prompt" if moba else "fox_prompt")(safe, *ins)


def _moba_static_cols(seq, n_heads):
    pos = jnp.arange(seq, dtype=jnp.int32)
    slopes = jnp.exp2(-8.0 * jnp.arange(1, n_heads + 1, dtype=F32) / n_heads) * LOG2E
    qparts = _split3(-(slopes[None, :] * pos.astype(F32)[:, None]))
    sparts = _split3(slopes)
    qx = jnp.zeros((seq, n_heads, LANE), F32)
    for c in range(3):
        qx = qx.at[:, :, COL_QPOS + c].set(qparts[c])
        qx = qx.at[:, :, COL_KPOS + c].set(jnp.broadcast_to(sparts[c][None, :], (seq, n_heads)))
        qx = qx.at[:, :, COL_KPOS + 3 + c].set(jnp.broadcast_to(sparts[c][None, :], (seq, n_heads)))
    blk = pos // MOBA_BLOCK
    lane = jnp.arange(LANE, dtype=jnp.int32)[None, :]
    kx = jnp.where(lane == blk[:, None], 1.0, 0.0)
    kx = jnp.where((lane >= COL_QPOS) & (lane < COL_QPOS + 3), 1.0, kx)
    kx = jnp.where((lane >= COL_KPOS) & (lane < COL_KPOS + 3), (blk * MOBA_BLOCK).astype(F32)[:, None], kx)
    kx = jnp.where((lane >= COL_KPOS + 3) & (lane < COL_KPOS + 6), (pos % MOBA_BLOCK).astype(F32)[:, None], kx)
    return qx.reshape(seq, n_heads * LANE).astype(BF16), kx.astype(BF16)


def _page_specs(n, block, layer_arg, n_pages, pages_per_step):
    def spec(i):
        def imap(b, s, pt):
            return (layer_arg, pt[b * n_pages + s * pages_per_step + i]) + (0,) * (len(block) - 2)
        return pl.BlockSpec(block, imap)
    return [spec(i) for i in range(n)]


def _cum_sample_body(pt_ref, *refs, pg):
    lf_refs, new_ref, c_ref, cnew_ref, carry_ref = refs[:pg], refs[pg], refs[pg + 1], refs[pg + 2], refs[pg + 3]
    s = pl.program_id(1)

    @pl.when(s == 0)
    def _():
        carry_ref[...] = jnp.zeros_like(carry_ref)

    triu = (lax.broadcasted_iota(jnp.int32, (LANE, LANE), 0)
            <= lax.broadcasted_iota(jnp.int32, (LANE, LANE), 1)).astype(F32)
    carry = carry_ref[...]
    nh = carry.shape[0]
    local = jnp.dot(jnp.concatenate([r[...] for r in lf_refs], axis=0), triu,
                    preferred_element_type=F32, precision=lax.Precision.HIGHEST)
    for i in range(pg):
        loc = local[i * nh:(i + 1) * nh, :]
        c_ref[:, i * LANE:(i + 1) * LANE] = loc + carry
        carry = carry + jnp.broadcast_to(loc[:, LANE - 1:LANE], loc.shape)
    carry_ref[...] = carry

    @pl.when(s == pl.num_programs(1) - 1)
    def _():
        cnew_ref[...] = jnp.dot(new_ref[...], triu, preferred_element_type=F32,
                                precision=lax.Precision.HIGHEST) + carry


def _cum_sample(page_table, logf_t, lfnew_t, *, n_heads):
    db, n_pages = page_table.shape
    page = logf_t.shape[2]
    assert page == LANE
    pg = math.gcd(CUMSUM_PAGES_PER_STEP, n_pages)
    in_specs = _page_specs(pg, (None, None, n_heads, page), 0, n_pages, pg)
    in_specs.append(pl.BlockSpec((None, n_heads, LANE), lambda b, s, pt: (b, 0, 0)))
    grid_spec = pltpu.PrefetchScalarGridSpec(
        num_scalar_prefetch=1, grid=(db, n_pages // pg), in_specs=in_specs,
        out_specs=[pl.BlockSpec((None, n_heads, pg * page), lambda b, s, pt: (b, 0, s)),
                   pl.BlockSpec((None, n_heads, LANE), lambda b, s, pt: (b, 0, 0))],
        scratch_shapes=[pltpu.VMEM((n_heads, LANE), F32)])
    return pl.pallas_call(
        functools.partial(_cum_sample_body, pg=pg), grid_spec=grid_spec,
        out_shape=[jax.ShapeDtypeStruct((db, n_heads, n_pages * page), F32),
                   jax.ShapeDtypeStruct((db, n_heads, LANE), F32)],
        compiler_params=_cparams(("arbitrary", "arbitrary")),
        name="fox_sample_cumsum")(page_table.reshape(-1), *([logf_t[None]] * pg), lfnew_t)


def _row_token(shape, n_heads, dec_seq):
    row = lax.broadcasted_iota(jnp.int32, shape, 0)
    tok = jnp.zeros(shape, jnp.int32)
    for t in range(1, dec_seq):
        tok = tok + (row >= t * n_heads).astype(jnp.int32)
    return tok


def _stage_pages(k_refs, v_refs, kb, vb, page, n_kv, want_ksum=False):
    hd = kb.shape[1] // n_kv
    ksums = []
    for i, (k_ref, v_ref) in enumerate(zip(k_refs, v_refs)):
        parts = []
        for h in range(n_kv):
            kf = k_ref[pl.ds(h, page, stride=n_kv), :]
            kb[i * page:(i + 1) * page, h * hd:(h + 1) * hd] = kf.astype(BF16)
            vb[i * page:(i + 1) * page, h * hd:(h + 1) * hd] = (
                v_ref[pl.ds(h, page, stride=n_kv), :].astype(BF16))
            if want_ksum:
                parts.append(jnp.sum(kf, axis=0, keepdims=True))
        if want_ksum:
            ksums.append(jnp.concatenate(parts, axis=1))
    return ksums


def _fox_sample_body(pt_ref, *refs, pg, page, dec_seq, n_heads, n_kv):
    k_refs, v_refs = refs[:pg], refs[pg:2 * pg]
    (q_ref, cq_ref, c_ref, cnew_ref, kn_ref, vn_ref, o_ref, kb, vb, m_sc, l_sc, acc_sc) = refs[2 * pg:]
    s = pl.program_id(1)

    @pl.when(s == 0)
    def _():
        m_sc[...] = jnp.full(m_sc.shape, -jnp.inf, F32)
        l_sc[...] = jnp.zeros(l_sc.shape, F32)
        acc_sc[...] = jnp.zeros(acc_sc.shape, F32)

    def update(logits, v):
        m_prev = m_sc[...]
        m_new = jnp.maximum(m_prev, jnp.max(logits, axis=1, keepdims=True))
        alpha = jnp.exp2(m_prev - m_new)
        pr = jnp.exp2(logits - m_new)
        l_sc[...] = alpha * l_sc[...] + jnp.sum(pr, axis=1, keepdims=True)
        acc_sc[...] = alpha * acc_sc[...] + jnp.dot(pr.astype(BF16), v, preferred_element_type=F32)
        m_sc[...] = m_new

    q = q_ref[...]
    cq = cq_ref[...]
    _stage_pages(k_refs, v_refs, kb, vb, page, n_kv)
    ck = jnp.concatenate([c_ref[...]] * dec_seq, axis=0)
    update(_dot_nt(q, kb[...]) + (cq - ck) * LOG2E, vb[...])

    @pl.when(s == pl.num_programs(1) - 1)
    def _():
        rows = q.shape[0]
        ckn = jnp.concatenate([cnew_ref[...]] * dec_seq, axis=0)
        logits = _dot_nt(q, kn_ref[...].astype(BF16)) + (cq - ckn) * LOG2E
        tok = _row_token((rows, LANE), n_heads, dec_seq)
        logits = jnp.where(lax.broadcasted_iota(jnp.int32, (rows, LANE), 1) <= tok, logits, -jnp.inf)
        update(logits, vn_ref[...].astype(BF16))
        o_ref[...] = acc_sc[...] / l_sc[...]


def _moba_sample_body(pt_ref, *refs, pg, page, dec_seq, n_heads, n_kv, past):
    k_refs, v_refs = refs[:pg], refs[pg:2 * pg]
    (q_ref, q32_ref, slope_ref, kn_ref, vn_ref, o_ref,
     kb, vb, acc_all, m_all, l_all, km) = refs[2 * pg:]
    s = pl.program_id(1)
    nblk = past // MOBA_BLOCK
    ppb = MOBA_BLOCK // page
    rows = q_ref.shape[0]
    lane = lax.broadcasted_iota(jnp.int32, (rows, LANE), 1)
    qpos = (past + _row_token((rows, 1), n_heads, dec_seq)).astype(F32)
    km_row = lax.broadcasted_iota(jnp.int32, km.shape, 0)

    @pl.when(s == 0)
    def _():
        m_all[...] = jnp.full(m_all.shape, -jnp.inf, F32)
        l_all[...] = jnp.zeros(l_all.shape, F32)
        km[...] = jnp.zeros(km.shape, F32)

    def block_partial(logits, v, blk):
        mj = jnp.max(logits, axis=1, keepdims=True)
        pr = jnp.exp2(logits - mj)
        acc_all[blk] = jnp.dot(pr.astype(BF16), v, preferred_element_type=F32)
        m_all[...] = jnp.where(lane == blk, mj, m_all[...])
        l_all[...] = jnp.where(lane == blk, jnp.sum(pr, axis=1, keepdims=True), l_all[...])

    q = q_ref[...]
    slope = slope_ref[...]
    ksums = _stage_pages(k_refs, v_refs, kb, vb, page, n_kv, want_ksum=True)
    width = pg * page
    kpos = (s * width + lax.broadcasted_iota(jnp.int32, (1, width), 1)).astype(F32)
    logits = _dot_nt(q, kb[...]) - slope * (qpos - kpos)
    for jb in range(pg // ppb):
        blk = s * (pg // ppb) + jb
        block_partial(logits[:, jb * MOBA_BLOCK:(jb + 1) * MOBA_BLOCK],
                      vb[jb * MOBA_BLOCK:(jb + 1) * MOBA_BLOCK, :], blk)
        ksum = ksums[jb * ppb]
        for i in range(1, ppb):
            ksum = ksum + ksums[jb * ppb + i]
        km[...] = jnp.where(km_row == blk, ksum * (1.0 / MOBA_BLOCK), km[...])

    @pl.when(s == pl.num_programs(1) - 1)
    def _():
        npos = (past + lax.broadcasted_iota(jnp.int32, (1, LANE), 1)).astype(F32)
        ln = _dot_nt(q, kn_ref[...].astype(BF16)) - slope * (qpos - npos)
        ln = jnp.where(npos <= qpos, ln, -jnp.inf)
        block_partial(ln, vn_ref[...].astype(BF16), nblk)
        gate = lax.dot_general(q32_ref[...], km[...], (((1,), (1,)), ((), ())),
                               preferred_element_type=F32, precision=lax.Precision.HIGHEST)
        allowed = _top3_allowed(gate, lane, jnp.full((rows, LANE), nblk, jnp.int32)) > 0.0
        m_row = jnp.max(jnp.where(allowed, m_all[...], -jnp.inf), axis=1, keepdims=True)
        w = jnp.where(allowed, jnp.exp2(m_all[...] - m_row), 0.0)
        denom = jnp.sum(w * l_all[...], axis=1, keepdims=True)
        out = jnp.zeros(o_ref.shape, F32)
        for jb in range(nblk + 1):
            out = out + w[:, jb:jb + 1] * acc_all[jb]
        o_ref[...] = out / denom


def _sample_attention(page_table, cache_k4, cache_v4, layer, qbd, kn_pad, vn_pad, extra, *, moba,
                      dec_seq, n_heads):
    db, n_pages = page_table.shape
    kd = qbd.shape[2]
    hd = cache_k4.shape[3]
    n_kv = kd // hd
    page = cache_k4.shape[2] // n_kv
    rows = qbd.shape[1]
    past = n_pages * page
    pg = math.gcd(PAGES_PER_STEP, n_pages)
    assert page == LANE and MOBA_BLOCK % page == 0 and pg % (MOBA_BLOCK // page) == 0
    assert past % MOBA_BLOCK == 0 and past // MOBA_BLOCK < LANE
    page_specs = (_page_specs(pg, (None, None, page * n_kv, hd), layer, n_pages, pg)
                  + _page_specs(pg, (None, None, page * n_kv, hd), layer, n_pages, pg))
    per_b = lambda a: pl.BlockSpec((None,) + a.shape[1:], lambda b, s, pt: (b,) + (0,) * (a.ndim - 1))
    common = [pltpu.VMEM((pg * page, kd), BF16), pltpu.VMEM((pg * page, kd), BF16)]
    if moba:
        q32, slope = extra
        ins = [qbd, q32, slope, kn_pad, vn_pad]
        in_specs = page_specs + [per_b(qbd), per_b(q32),
                                 pl.BlockSpec(slope.shape, lambda b, s, pt: (0, 0)),
                                 per_b(kn_pad), per_b(vn_pad)]
        scratch = common + [pltpu.VMEM((past // MOBA_BLOCK + 1, rows, kd), F32),
                            pltpu.VMEM((rows, LANE), F32), pltpu.VMEM((rows, LANE), F32),
                            pltpu.VMEM((LANE, kd), F32)]
        body = functools.partial(_moba_sample_body, pg=pg, page=page, dec_seq=dec_seq,
                                 n_heads=n_heads, n_kv=n_kv, past=past)
    else:
        cq, c_t, cnew_t = extra
        ins = [qbd, cq, c_t, cnew_t, kn_pad, vn_pad]
        in_specs = page_specs + [per_b(qbd), per_b(cq),
                                 pl.BlockSpec((None, n_heads, pg * page), lambda b, s, pt: (b, 0, s)),
                                 per_b(cnew_t), per_b(kn_pad), per_b(vn_pad)]
        scratch = common + [pltpu.VMEM((rows, 1), F32), pltpu.VMEM((rows, 1), F32),
                            pltpu.VMEM((rows, kd), F32)]
        body = functools.partial(_fox_sample_body, pg=pg, page=page, dec_seq=dec_seq, n_heads=n_heads,
                                 n_kv=n_kv)
    grid_spec = pltpu.PrefetchScalarGridSpec(
        num_scalar_prefetch=1, grid=(db, n_pages // pg), in_specs=in_specs,
        out_specs=pl.BlockSpec((None, rows, kd), lambda b, s, pt: (b, 0, 0)), scratch_shapes=scratch)
    return pl.pallas_call(
        body, grid_spec=grid_spec, out_shape=jax.ShapeDtypeStruct((db, rows, kd), F32),
        compiler_params=_cparams(("arbitrary", "arbitrary")),
        name="moba_sample" if moba else "fox_sample")(
            page_table.reshape(-1), *([cache_k4] * pg), *([cache_v4] * pg), *ins)


def _wo_body(x_ref, mix_ref, w_ref, o_ref):
    o_ref[...] = x_ref[...] + jnp.dot(mix_ref[...], w_ref[...], preferred_element_type=F32)


def _wo_residual(x, mix_bf, wo_bf, layer, *, tm):
    rows, d = x.shape
    tm = _row_tile(rows, tm)
    return pl.pallas_call(
        _wo_body, grid=(rows // tm,),
        in_specs=[pl.BlockSpec((tm, d), lambda i: (i, 0)),
                  pl.BlockSpec((tm, mix_bf.shape[1]), lambda i: (i, 0)),
                  pl.BlockSpec((None,) + wo_bf.shape[1:], lambda i: (layer, 0, 0))],
        out_specs=pl.BlockSpec((tm, d), lambda i: (i, 0)),
        out_shape=jax.ShapeDtypeStruct((rows, d), F32),
        compiler_params=_cparams(("arbitrary",)), name="wo_residual")(x, mix_bf, wo_bf)


def _mlp_body(h_ref, g_ref, wu_ref, wd_ref, o_ref, hn_sc):
    f = pl.program_id(1)

    @pl.when(f == 0)
    def _():
        h = h_ref[...]
        ms = jnp.mean(h * h, axis=-1, keepdims=True)
        hn_sc[...] = (h * lax.rsqrt(ms + RMS_EPS) * g_ref[...]).astype(BF16)
        o_ref[...] = h

    u = jnp.maximum(jnp.dot(hn_sc[...], wu_ref[...].astype(BF16), preferred_element_type=F32), 0.0)
    o_ref[...] += jnp.dot((u * u).astype(BF16), wd_ref[...].astype(BF16), preferred_element_type=F32)


def _mlp_residual(h, g_mlp, w_up, w_down, layer, *, tm, tf):
    rows, d = h.shape
    dff = w_up.shape[2]
    tm = _row_tile(rows, tm)
    tf = min(tf, dff)
    assert dff % tf == 0
    return pl.pallas_call(
        _mlp_body, grid=(rows // tm, dff // tf),
        in_specs=[pl.BlockSpec((tm, d), lambda i, f: (i, 0)),
                  pl.BlockSpec(g_mlp.shape, lambda i, f: (0, 0)),
                  pl.BlockSpec((None, d, tf), lambda i, f: (layer, 0, f)),
                  pl.BlockSpec((None, tf, d), lambda i, f: (layer, f, 0))],
        out_specs=pl.BlockSpec((tm, d), lambda i, f: (i, 0)),
        out_shape=jax.ShapeDtypeStruct((rows, d), F32),
        scratch_shapes=[pltpu.VMEM((tm, d), BF16)],
        compiler_params=_cparams(("arbitrary", "arbitrary")),
        name="mlp_residual")(h, g_mlp, w_up, w_down)


def kernel(x_prompt, x_sample, cache_k, cache_v, cache_logf, page_table, attn_norm, w_qkv, q_norm, k_norm,
           w_fgate, b_fgate, w_o, mlp_norm, w_up, w_down):
    batch, seq, d = x_prompt.shape
    db, dec_seq, _ = x_sample.shape
    depth, n_pool, page, n_kv, hd = cache_k.shape
    n_heads = w_fgate.shape[2]
    group = n_heads // n_kv
    qd, kd = n_heads * hd, n_kv * hd
    n_pages = page_table.shape[1]
    past = n_pages * page
    rows_s = dec_seq * n_heads

    hp = x_prompt.reshape(batch * seq, d)
    hs = x_sample.reshape(db * dec_seq, d)
    cache_k4 = cache_k.reshape(depth, n_pool, page * n_kv, hd)
    cache_v4 = cache_v.reshape(depth, n_pool, page * n_kv, hd)
    row_kv = (jnp.arange(rows_s) % n_heads) // group
    row_mask = (row_kv[:, None] == jnp.arange(n_kv)[None, :]).astype(F32)
    slopes = jnp.exp2(-8.0 * jnp.arange(1, n_heads + 1, dtype=F32) / n_heads)

    def block_diag(q2d, dtype):
        q = q2d.astype(F32).reshape(db, rows_s, 1, hd) * row_mask[None, :, :, None]
        return q.reshape(db, rows_s, kd).astype(dtype)

    def pad_page(a2d):
        a = a2d.reshape(db, dec_seq, kd)
        return jnp.pad(a, ((0, 0), (0, LANE - dec_seq), (0, 0)))

    def own_lanes(o):
        o6 = o.reshape(db, dec_seq, n_kv, group, n_kv, hd)
        return jnp.stack([o6[:, :, k, :, k, :] for k in range(n_kv)], axis=2).reshape(db * dec_seq, qd)

    outs = {k: [] for k in ("kp", "vp", "fp", "ks", "vs", "fs")}
    w_qkv_bf, wo_bf = w_qkv.astype(BF16), w_o.astype(BF16)
    for i in range(depth):
        fox = i % 2 == 0
        j = i // 2
        g_attn, qg, kg = attn_norm[i][None, :], q_norm[i][None, :], k_norm[i][None, :]
        safe = _fixed_max_is_safe(qg, kg, hd)
        if fox:
            wf_bf = jnp.pad(w_fgate[j], ((0, 0), (0, LANE - n_heads))).astype(BF16)
            bf_pad = jnp.pad(b_fgate[j], (0, LANE - n_heads))[None, :]
        else:
            wf_bf = bf_pad = None
        proj = functools.partial(_qkv_proj, g_attn=g_attn, w_bf=w_qkv_bf, q_gain=qg, k_gain=kg, wf_bf=wf_bf,
                                 bf_pad=bf_pad, layer=i, n_heads=n_heads, n_kv=n_kv, hd=hd, q_f32=not fox)
        res_p = proj(hp, attn_copies=True, tm=512)
        res_s = proj(hs, attn_copies=False, tm=128)
        qp, kp32, vp32, kpb, vpt = res_p[:5]
        qs, ks32, vs32 = res_s[:3]
        outs["kp"].append(kp32.reshape(batch, seq, n_kv, hd))
        outs["vp"].append(vp32.reshape(batch, seq, n_kv, hd))
        outs["ks"].append(ks32.reshape(db, dec_seq, n_kv, hd))
        outs["vs"].append(vs32.reshape(db, dec_seq, n_kv, hd))
        kn_pad, vn_pad = pad_page(ks32), pad_page(vs32)
        if fox:
            lfp, lfp16, lfs16 = res_p[5], res_p[6], res_s[4]
            outs["fp"].append(lfp16.reshape(batch, seq, n_heads))
            outs["fs"].append(lfs16.reshape(db, dec_seq, n_heads))
            qx, kx = _fox_aug(lfp, batch=batch, seq=seq, n_heads=n_heads, n_kv=n_kv)
            mix_p = _flash_prompt(qp, qx, kpb, kx, vpt, None, safe, batch=batch, seq=seq, n_heads=n_heads,
                                  n_kv=n_kv, hd=hd, moba=False)
            logf_t = jnp.swapaxes(cache_logf[j], 1, 2)
            lfnew_t = jnp.pad(jnp.swapaxes(lfs16.reshape(db, dec_seq, n_heads), 1, 2),
                              ((0, 0), (0, 0), (0, LANE - dec_seq)))
            c_t, cnew_t = _cum_sample(page_table, logf_t, lfnew_t, n_heads=n_heads)
            cq = jnp.swapaxes(cnew_t[:, :, :dec_seq], 1, 2).reshape(db, rows_s, 1)
            o_s = _sample_attention(page_table, cache_k4, cache_v4, i, block_diag(qs, BF16), kn_pad, vn_pad,
                                    (cq, c_t, cnew_t), moba=False, dec_seq=dec_seq, n_heads=n_heads)
        else:
            kmean = _kmean(kp32, batch=batch, seq=seq, n_kv=n_kv)
            qx, kx = _moba_static_cols(seq, n_heads)
            mix_p = _flash_prompt(qp, qx, kpb, kx, vpt, kmean, safe, batch=batch, seq=seq, n_heads=n_heads,
                                  n_kv=n_kv, hd=hd, moba=True)
            slope_col = (jnp.tile(slopes, dec_seq) * LOG2E)[:, None]
            q_scaled = qs * (hd ** -0.5 * LOG2E)
            o_s = _sample_attention(page_table, cache_k4, cache_v4, i, block_diag(q_scaled, BF16), kn_pad,
                                    vn_pad, (block_diag(qs, F32), slope_col), moba=True, dec_seq=dec_seq,
                                    n_heads=n_heads)
        mix_s = own_lanes(o_s).astype(BF16)
        g_mlp = mlp_norm[i][None, :]
        hp = _wo_residual(hp, mix_p, wo_bf, i, tm=512)
        hs = _wo_residual(hs, mix_s, wo_bf, i, tm=128)
        hp = _mlp_residual(hp, g_mlp, w_up, w_down, i, tm=1024, tf=512)
        hs = _mlp_residual(hs, g_mlp, w_up, w_down, i, tm=128, tf=512)
    return (hp.reshape(batch, seq, d), hs.reshape(db, dec_seq, d),
            jnp.stack(outs["kp"]), jnp.stack(outs["vp"]), jnp.stack(outs["fp"]),
            jnp.stack(outs["ks"]), jnp.stack(outs["vs"]), jnp.stack(outs["fs"]))
```
